```python
import math
import jax, jax.numpy as jnp
from jax import lax
import numpy as np

D_MODEL = 1024
BATCH = 32
SEQ = 2048
DEPTH = 4
DEC_BATCH = 8
DEC_SEQ = 32
PAST_LEN = 4096

CHUNK = 64
N_MIXERS = 2
N_SSM = (DEPTH + 1) // 2
N_SB = DEPTH // 2
EXPAND = 2
SSM_WIDTH = EXPAND * D_MODEL
SSM_GROUP = 16
SSM_GROUPS = SSM_WIDTH // SSM_GROUP
SSM_STATE = 64
SB_WIDTH = EXPAND * D_MODEL
SB_HEAD_DIM = 128
SB_HEADS = SB_WIDTH // SB_HEAD_DIM
Q_BLOCK = 128
RMS_EPS = 1e-6
DT_MIN = 1e-3
DT_MAX = 1e-1

kernel_name = "streaming_s5_stickbreaking_hybrid_step"


def rms_norm(x, g):
    xf = x.astype(jnp.float32)
    y = xf * lax.rsqrt(jnp.mean(xf * xf, axis=-1, keepdims=True) + RMS_EPS)
    return (y * g.astype(jnp.float32)).astype(x.dtype)


def s5_discretize(a_re, a_im, log_step):
    f32 = jnp.float32
    ar = a_re.astype(f32)
    ai = a_im.astype(f32)
    dt = jnp.exp(log_step.astype(f32))[:, None]
    mag = jnp.exp(ar * dt)
    ph = ai * dt
    lb_re = mag * jnp.cos(ph)
    lb_im = mag * jnp.sin(ph)
    nr = lb_re - 1.0
    ni = lb_im
    den = ar * ar + ai * ai
    fac_re = (nr * ar + ni * ai) / den
    fac_im = (ni * ar - nr * ai) / den
    return lb_re, lb_im, fac_re, fac_im


def _ssm_combine(e1, e2):
    a1r, a1i, b1r, b1i = e1
    a2r, a2i, b2r, b2i = e2
    ar = a2r * a1r - a2i * a1i
    ai = a2r * a1i + a2i * a1r
    br = a2r * b1r - a2i * b1i + b2r
    bi = a2r * b1i + a2i * b1r + b2i
    return ar, ai, br, bi


def s5_mixer(u, h0_re, h0_im, a_re, a_im, log_step, b_re, b_im, c_re, c_im, d_skip):
    f32 = jnp.float32
    n, t, _ = u.shape
    blk = min(CHUNK, t)
    nblk = t // blk
    lb_re, lb_im, fac_re, fac_im = s5_discretize(a_re, a_im, log_step)
    br = b_re.astype(f32)
    bi = b_im.astype(f32)
    bbar_re = fac_re[..., None] * br - fac_im[..., None] * bi
    bbar_im = fac_re[..., None] * bi + fac_im[..., None] * br
    cr = c_re.astype(f32)
    ci = c_im.astype(f32)
    uf = u.astype(f32)
    ug = uf.reshape(n, nblk, blk, SSM_GROUPS, SSM_GROUP).transpose(1, 0, 2, 3, 4)

    def step(carry, u_c):
        hr, hi = carry
        bu_re = jnp.einsum('nlgh,gph->nlgp', u_c, bbar_re)
        bu_im = jnp.einsum('nlgh,gph->nlgp', u_c, bbar_im)
        bu_re = bu_re.at[:, 0].add(lb_re * hr - lb_im * hi)
        bu_im = bu_im.at[:, 0].add(lb_re * hi + lb_im * hr)
        a_r = jnp.broadcast_to(lb_re, bu_re.shape)
        a_i = jnp.broadcast_to(lb_im, bu_im.shape)
        _, _, sr, si = lax.associative_scan(_ssm_combine, (a_r, a_i, bu_re, bu_im), axis=1)
        y = jnp.einsum('nlgp,ghp->nlgh', sr, cr) - jnp.einsum('nlgp,ghp->nlgh', si, ci)
        return (sr[:, -1], si[:, -1]), y

    (hr, hi), ys = lax.scan(step, (h0_re.astype(f32), h0_im.astype(f32)), ug)
    y = ys.transpose(1, 0, 2, 3, 4).reshape(n, t, SSM_WIDTH) + d_skip.astype(f32) * uf
    return y.astype(u.dtype), hr, hi


def ssm_layer(x, h0_re, h0_im, g_pre, g_post, w_in, a_re, a_im, log_step,
              b_re, b_im, c_re, c_im, d_skip, w_glu, w_out):
    h = rms_norm(x, g_pre)
    proj = h @ w_in
    u, gate = jnp.split(proj, 2, axis=-1)
    y, hr, hi = s5_mixer(u, h0_re, h0_im, a_re, a_im, log_step, b_re, b_im, c_re, c_im, d_skip)
    y = jax.nn.gelu(y)
    y = y * jax.nn.sigmoid(y @ w_glu)
    o = (y * jax.nn.silu(gate)) @ w_out
    return x + rms_norm(o, g_post), hr, hi


def stick_breaking(q, k, v, q_offset):
    f32 = jnp.float32
    n, hh, t, dh = q.shape
    s = k.shape[2]
    blk = min(Q_BLOCK, t)
    nblk = t // blk
    kf = k.astype(f32)
    vf = v.astype(f32)
    kpos = jnp.arange(s)
    qb = (q.astype(f32) * (dh ** -0.5)).reshape(n, hh, nblk, blk, dh).transpose(2, 0, 1, 3, 4)
    qpos = (q_offset + jnp.arange(t)).reshape(nblk, blk)

    def one_block(args):
        qi, pi = args
        z = jnp.einsum('nhqd,nhkd->nhqk', qi, kf)
        valid = kpos[None, :] < pi[:, None]
        log_keep = jnp.where(valid, jax.nn.log_sigmoid(-z), 0.0)
        log_rest = lax.cumsum(log_keep, axis=3, reverse=True) - log_keep
        w = jnp.where(valid, jnp.exp(jax.nn.log_sigmoid(z) + log_rest), 0.0)
        return jnp.einsum('nhqk,nhkd->nhqd', w, vf)

    o = lax.map(one_block, (qb, qpos))
    return o.transpose(1, 2, 0, 3, 4).reshape(n, hh, t, dh).astype(q.dtype)


def sb_layer(x, k_past, v_past, g_pre, g_post, w_in, w_out):
    n, t, _ = x.shape
    h = rms_norm(x, g_pre)
    proj = h @ w_in
    q, k, v, gate = jnp.split(proj, 4, axis=-1)

    def heads(a):
        return a.reshape(n, t, SB_HEADS, SB_HEAD_DIM).transpose(0, 2, 1, 3)

    q, k, v = heads(q), heads(k), heads(v)
    past = k_past.shape[2]
    k_all = jnp.concatenate([k_past.astype(k.dtype), k], axis=2)
    v_all = jnp.concatenate([v_past.astype(v.dtype), v], axis=2)
    o = stick_breaking(q, k_all, v_all, past)
    o = o.transpose(0, 2, 1, 3).reshape(n, t, SB_WIDTH)
    o = (o * jax.nn.silu(gate)) @ w_out
    return x + rms_norm(o, g_post), k, v


def run_trunk(x, h0_re, h0_im, k_past, v_past, norm_pre, norm_post, w_in_ssm, ssm_a_re, ssm_a_im,
              ssm_log_step, ssm_b_re, ssm_b_im, ssm_c_re, ssm_c_im, ssm_d, w_glu, w_out_ssm,
              w_in_sb, w_out_sb):
    new_k, new_v, new_re, new_im = [], [], [], []
    for i in range(DEPTH):
        j = i // N_MIXERS
        if i % N_MIXERS == 0:
            x, hr, hi = ssm_layer(x, h0_re[j], h0_im[j], norm_pre[i], norm_post[i], w_in_ssm[j],
                                  ssm_a_re[j], ssm_a_im[j], ssm_log_step[j], ssm_b_re[j], ssm_b_im[j],
                                  ssm_c_re[j], ssm_c_im[j], ssm_d[j], w_glu[j], w_out_ssm[j])
            new_re.append(hr)
            new_im.append(hi)
        else:
            x, k, v = sb_layer(x, k_past[j], v_past[j], norm_pre[i], norm_post[i], w_in_sb[j], w_out_sb[j])
            new_k.append(k)
            new_v.append(v)
    return x, jnp.stack(new_k), jnp.stack(new_v), jnp.stack(new_re), jnp.stack(new_im)


def setup_inputs(seed: int = 0) -> dict:
    key = jax.random.key(seed)
    ks = jax.random.split(key, 24)
    f32 = jnp.float32
    nrm = lambda k, shape, s: jax.random.normal(k, shape, f32) * s
    n_idx = jnp.arange(SSM_STATE, dtype=f32)
    a_re = -0.5 + nrm(ks[0], (N_SSM, SSM_GROUPS, SSM_STATE), 0.01)
    a_im = math.pi * n_idx + nrm(ks[1], (N_SSM, SSM_GROUPS, SSM_STATE), 0.01)
    log_step = jax.random.uniform(ks[2], (N_SSM, SSM_GROUPS), f32, math.log(DT_MIN), math.log(DT_MAX))
    return {
        "x_prompt": nrm(ks[3], (BATCH, SEQ, D_MODEL), 1.0),
        "x_sample": nrm(ks[4], (DEC_BATCH, DEC_SEQ, D_MODEL), 1.0),
        "cache_sb_k": nrm(ks[5], (N_SB, DEC_BATCH, SB_HEADS, PAST_LEN, SB_HEAD_DIM), 1.0),
        "cache_sb_v": nrm(ks[6], (N_SB, DEC_BATCH, SB_HEADS, PAST_LEN, SB_HEAD_DIM), 1.0),
        "state_ssm_re": nrm(ks[7], (N_SSM, DEC_BATCH, SSM_GROUPS, SSM_STATE), 0.1),
        "state_ssm_im": nrm(ks[8], (N_SSM, DEC_BATCH, SSM_GROUPS, SSM_STATE), 0.1),
        "norm_pre": 1.0 + nrm(ks[9], (DEPTH, D_MODEL), 0.02),
        "norm_post": 1.0 + nrm(ks[10], (DEPTH, D_MODEL), 0.02),
        "w_in_ssm": nrm(ks[11], (N_SSM, D_MODEL, 2 * SSM_WIDTH), D_MODEL ** -0.5),
        "ssm_a_re": a_re,
        "ssm_a_im": a_im,
        "ssm_log_step": log_step,
        "ssm_b_re": nrm(ks[12], (N_SSM, SSM_GROUPS, SSM_STATE, SSM_GROUP), (2 * SSM_GROUP) ** -0.5),
        "ssm_b_im": nrm(ks[13], (N_SSM, SSM_GROUPS, SSM_STATE, SSM_GROUP), (2 * SSM_GROUP) ** -0.5),
        "ssm_c_re": nrm(ks[14], (N_SSM, SSM_GROUPS, SSM_GROUP, SSM_STATE), (2 * SSM_STATE) ** -0.5),
        "ssm_c_im": nrm(ks[15], (N_SSM, SSM_GROUPS, SSM_GROUP, SSM_STATE), (2 * SSM_STATE) ** -0.5),
        "ssm_d": nrm(ks[16], (N_SSM, SSM_WIDTH), 0.5),
        "w_glu": nrm(ks[17], (N_SSM, SSM_WIDTH, SSM_WIDTH), SSM_WIDTH ** -0.5),
        "w_out_ssm": nrm(ks[18], (N_SSM, SSM_WIDTH, D_MODEL), SSM_WIDTH ** -0.5),
        "w_in_sb": nrm(ks[19], (N_SB, D_MODEL, 4 * SB_WIDTH), D_MODEL ** -0.5),
        "w_out_sb": nrm(ks[20], (N_SB, SB_WIDTH, D_MODEL), SB_WIDTH ** -0.5),
    }


def reference(x_prompt, x_sample, cache_sb_k, cache_sb_v, state_ssm_re, state_ssm_im,
              norm_pre, norm_post, w_in_ssm, ssm_a_re, ssm_a_im, ssm_log_step, ssm_b_re, ssm_b_im,
              ssm_c_re, ssm_c_im, ssm_d, w_glu, w_out_ssm, w_in_sb, w_out_sb):
    nb = x_prompt.shape[0]
    h0_re = jnp.zeros((N_SSM, nb, SSM_GROUPS, SSM_STATE), jnp.float32)
    h0_im = jnp.zeros((N_SSM, nb, SSM_GROUPS, SSM_STATE), jnp.float32)
    kv0 = jnp.zeros((N_SB, nb, SB_HEADS, 0, SB_HEAD_DIM), x_prompt.dtype)
    y_prompt, k_prompt, v_prompt, ssm_re_prompt, ssm_im_prompt = run_trunk(
        x_prompt, h0_re, h0_im, kv0, kv0, norm_pre, norm_post, w_in_ssm, ssm_a_re, ssm_a_im,
        ssm_log_step, ssm_b_re, ssm_b_im, ssm_c_re, ssm_c_im, ssm_d, w_glu, w_out_ssm, w_in_sb, w_out_sb)
    y_sample, k_sample, v_sample, ssm_re_sample, ssm_im_sample = run_trunk(
        x_sample, state_ssm_re, state_ssm_im, cache_sb_k, cache_sb_v, norm_pre, norm_post, w_in_ssm,
        ssm_a_re, ssm_a_im, ssm_log_step, ssm_b_re, ssm_b_im, ssm_c_re, ssm_c_im, ssm_d, w_glu,
        w_out_ssm, w_in_sb, w_out_sb)
    return (y_prompt, y_sample, k_prompt, v_prompt, ssm_re_prompt, ssm_im_prompt,
            k_sample, v_sample, ssm_re_sample, ssm_im_sample)
```

```python
import functools
import math

import jax
import jax.numpy as jnp
from jax import lax
from jax.experimental import pallas as pl
from jax.experimental.pallas import tpu as pltpu

F32 = jnp.float32
BF16 = jnp.bfloat16

RMS_EPS = 1e-6
SSM_GROUP = 16
SSM_STATE = 64
HEAD_DIM = 128

V7X_LANES = 128
V7X_VMEM_BYTES = 64 * 1024 * 1024

SSM_CH_TILE = V7X_LANES
SSM_G_TILE = SSM_CH_TILE // SSM_GROUP
SSM_S_TILE = SSM_G_TILE * SSM_STATE
SCAN_COLS = V7X_LANES


def _compiler_params(semantics, vmem_mib):
    assert vmem_mib * 1024 * 1024 < V7X_VMEM_BYTES
    return pltpu.CompilerParams(dimension_semantics=semantics,
                                vmem_limit_bytes=vmem_mib * 1024 * 1024)


def _resident(shape):
    zeros = (0,) * len(shape)
    return pl.BlockSpec(shape, lambda *_: zeros, pipeline_mode=pl.Buffered(1))


def _rms(x, g):
    return x * lax.rsqrt(jnp.mean(x * x, axis=-1, keepdims=True) + RMS_EPS) * g


def _silu(x):
    return x * jax.nn.sigmoid(x)


def _ssm_in_kernel(x_ref, g_ref, wu_ref, wg_ref, u_ref, gate_ref):
    xn = _rms(x_ref[...], g_ref[...]).astype(BF16)
    u_ref[...] = jnp.dot(xn, wu_ref[...], preferred_element_type=F32)
    gate_ref[...] = jnp.dot(xn, wg_ref[...], preferred_element_type=F32)


def _ssm_in_proj(x, g, wu, wg, tc):
    n, t, d = x.shape
    w = wu.shape[1]
    return pl.pallas_call(
        _ssm_in_kernel,
        grid=(n, t // tc),
        in_specs=[pl.BlockSpec((None, tc, d), lambda i, c: (i, c, 0)),
                  _resident((1, d)), _resident((d, w)), _resident((d, w))],
        out_specs=[pl.BlockSpec((tc, w), lambda i, c: (c, i)),
                   pl.BlockSpec((None, tc, w), lambda i, c: (i, c, 0))],
        out_shape=[jax.ShapeDtypeStruct((t, n * w), F32),
                   jax.ShapeDtypeStruct((n, t, w), F32)],
        compiler_params=_compiler_params(("parallel", "parallel"), 48),
        name="ssm_in_proj",
    )(x, g, wu, wg)


def _ssm_mixer_kernel(u_ref, h0r_ref, h0i_ref, bm_ref, cm_ref, lbr_ref, lbi_ref, d_ref,
                      y_ref, htr_ref, hti_ref, sr_ref, si_ref, bu_ref, hs_ref, *, tc, n):
    c = pl.program_id(1)

    @pl.when(c == 0)
    def _():
        sr_ref[...] = h0r_ref[...]
        si_ref[...] = h0i_ref[...]

    u2 = u_ref[...].reshape(tc * n, SSM_CH_TILE)
    bu_ref[...] = jnp.dot(u2.astype(BF16), bm_ref[...], preferred_element_type=F32)

    for q in range(SSM_S_TILE // SCAN_COLS):
        re_cols = slice(q * SCAN_COLS, (q + 1) * SCAN_COLS)
        im_cols = slice(SSM_S_TILE + q * SCAN_COLS, SSM_S_TILE + (q + 1) * SCAN_COLS)
        lr = jnp.broadcast_to(lbr_ref[:, re_cols], (n, SCAN_COLS))
        li = jnp.broadcast_to(lbi_ref[:, re_cols], (n, SCAN_COLS))

        def step(t, carry, re_cols=re_cols, im_cols=im_cols, lr=lr, li=li):
            hr, hi = carry
            rows = pl.ds(pl.multiple_of(t * n, n), n)
            nr = lr * hr - li * hi + bu_ref[rows, re_cols]
            ni = lr * hi + li * hr + bu_ref[rows, im_cols]
            hs_ref[rows, re_cols] = nr.astype(hs_ref.dtype)
            hs_ref[rows, im_cols] = ni.astype(hs_ref.dtype)
            return nr, ni

        hr, hi = lax.fori_loop(0, tc, step, (sr_ref[:, re_cols], si_ref[:, re_cols]),
                               unroll=8)
        sr_ref[:, re_cols] = hr
        si_ref[:, re_cols] = hi

    y = jnp.dot(hs_ref[...].astype(BF16), cm_ref[...], preferred_element_type=F32)
    y = y + d_ref[...] * u2
    y_ref[...] = y.reshape(tc, n, SSM_CH_TILE)

    @pl.when(c == pl.num_programs(1) - 1)
    def _():
        htr_ref[...] = sr_ref[...]
        hti_ref[...] = si_ref[...]


def _ssm_mixer(u_tm, h0r, h0i, bm, cm, lbr, lbi, dskip, tc):
    t, n, w = u_tm.shape
    tiles = w // SSM_CH_TILE
    hs_dtype = BF16 if n % 16 == 0 else F32
    kernel = functools.partial(_ssm_mixer_kernel, tc=tc, n=n)
    return pl.pallas_call(
        kernel,
        grid=(tiles, t // tc),
        in_specs=[pl.BlockSpec((tc, n, SSM_CH_TILE), lambda j, c: (c, 0, j)),
                  pl.BlockSpec((n, SSM_S_TILE), lambda j, c: (0, j)),
                  pl.BlockSpec((n, SSM_S_TILE), lambda j, c: (0, j)),
                  pl.BlockSpec((None, SSM_CH_TILE, 2 * SSM_S_TILE), lambda j, c: (j, 0, 0)),
                  pl.BlockSpec((None, 2 * SSM_S_TILE, SSM_CH_TILE), lambda j, c: (j, 0, 0)),
                  pl.BlockSpec((None, 1, SSM_S_TILE), lambda j, c: (j, 0, 0)),
                  pl.BlockSpec((None, 1, SSM_S_TILE), lambda j, c: (j, 0, 0)),
                  pl.BlockSpec((1, SSM_CH_TILE), lambda j, c: (0, j))],
        out_specs=[pl.BlockSpec((tc, n, SSM_CH_TILE), lambda j, c: (c, 0, j)),
                   pl.BlockSpec((n, SSM_S_TILE), lambda j, c: (0, j)),
                   pl.BlockSpec((n, SSM_S_TILE), lambda j, c: (0, j))],
        out_shape=[jax.ShapeDtypeStruct((t, n, w), F32),
                   jax.ShapeDtypeStruct(h0r.shape, F32),
                   jax.ShapeDtypeStruct(h0i.shape, F32)],
        scratch_shapes=[pltpu.VMEM((n, SSM_S_TILE), F32),
                        pltpu.VMEM((n, SSM_S_TILE), F32),
                        pltpu.VMEM((tc * n, 2 * SSM_S_TILE), F32),
                        pltpu.VMEM((tc * n, 2 * SSM_S_TILE), hs_dtype)],
        compiler_params=_compiler_params(("parallel", "arbitrary"), 48),
        name="ssm_mixer",
    )(u_tm, h0r, h0i, bm, cm, lbr, lbi, dskip)


def _ssm_out_kernel(y_ref, gate_ref, x_ref, wglu_ref, wout_ref, g_ref, o_ref):
    y = jax.nn.gelu(y_ref[...])
    z = jnp.dot(y.astype(BF16), wglu_ref[...], preferred_element_type=F32)
    y = y * jax.nn.sigmoid(z)
    a = (y * _silu(gate_ref[...])).astype(BF16)
    o = jnp.dot(a, wout_ref[...], preferred_element_type=F32)
    o_ref[...] = x_ref[...] + _rms(o, g_ref[...])


def _ssm_out_proj(y_tm2, gate, x, wglu, wout, g, tc):
    n, t, d = x.shape
    w = gate.shape[2]
    return pl.pallas_call(
        _ssm_out_kernel,
        grid=(n, t // tc),
        in_specs=[pl.BlockSpec((tc, w), lambda i, c: (c, i)),
                  pl.BlockSpec((None, tc, w), lambda i, c: (i, c, 0)),
                  pl.BlockSpec((None, tc, d), lambda i, c: (i, c, 0)),
                  _resident((w, w)), _resident((w, d)), _resident((1, d))],
        out_specs=pl.BlockSpec((None, tc, d), lambda i, c: (i, c, 0)),
        out_shape=jax.ShapeDtypeStruct((n, t, d), F32),
        compiler_params=_compiler_params(("parallel", "parallel"), 48),
        name="ssm_out_proj",
    )(y_tm2, gate, x, wglu, wout, g)


def _ssm_weights(a_re, a_im, log_step, b_re, b_im, c_re, c_im):
    ar = a_re.astype(F32)
    ai = a_im.astype(F32)
    dt = jnp.exp(log_step.astype(F32))[:, None]
    mag = jnp.exp(ar * dt)
    ph = ai * dt
    lb_re = mag * jnp.cos(ph)
    lb_im = mag * jnp.sin(ph)
    nr = lb_re - 1.0
    ni = lb_im
    den = ar * ar + ai * ai
    fac_re = (nr * ar + ni * ai) / den
    fac_im = (ni * ar - nr * ai) / den
    br = b_re.astype(F32)
    bi = b_im.astype(F32)
    bbar_re = fac_re[..., None] * br - fac_im[..., None] * bi
    bbar_im = fac_re[..., None] * bi + fac_im[..., None] * br

    groups = a_re.shape[0]
    tiles = groups // SSM_G_TILE
    eye = jnp.eye(SSM_G_TILE, dtype=F32)

    def b_tile(b):
        b = b.reshape(tiles, SSM_G_TILE, SSM_STATE, SSM_GROUP).transpose(0, 1, 3, 2)
        blk = b[:, :, :, None, :] * eye[None, :, None, :, None]
        return blk.reshape(tiles, SSM_CH_TILE, SSM_S_TILE)

    def c_tile(cc):
        cc = cc.reshape(tiles, SSM_G_TILE, SSM_GROUP, SSM_STATE).transpose(0, 1, 3, 2)
        blk = cc[:, :, :, None, :] * eye[None, :, None, :, None]
        return blk.reshape(tiles, SSM_S_TILE, SSM_CH_TILE)

    bm = jnp.concatenate([b_tile(bbar_re), b_tile(bbar_im)], axis=2).astype(BF16)
    cm = jnp.concatenate([c_tile(c_re.astype(F32)), -c_tile(c_im.astype(F32))], axis=1).astype(BF16)
    lbr = lb_re.reshape(tiles, 1, SSM_S_TILE)
    lbi = lb_im.reshape(tiles, 1, SSM_S_TILE)
    return bm, cm, lbr, lbi


def _ssm_layer(x, h0r, h0i, g_pre, g_post, w_in, a_re, a_im, log_step, b_re, b_im,
               c_re, c_im, d_skip, w_glu, w_out, row_tile, scan_tile):
    n, t, _ = x.shape
    w = w_glu.shape[0]
    wu = w_in[:, :w].astype(BF16)
    wg = w_in[:, w:].astype(BF16)
    bm, cm, lbr, lbi = _ssm_weights(a_re, a_im, log_step, b_re, b_im, c_re, c_im)
    u_tm2, gate = _ssm_in_proj(x, g_pre[None, :], wu, wg, row_tile)
    y_tm, htr, hti = _ssm_mixer(u_tm2.reshape(t, n, w), h0r.reshape(n, -1), h0i.reshape(n, -1),
                                bm, cm, lbr, lbi, d_skip[None, :], scan_tile)
    x = _ssm_out_proj(y_tm.reshape(t, n * w), gate, x, w_glu.astype(BF16), w_out.astype(BF16),
                      g_post[None, :], row_tile)
    return x, htr.reshape(h0r.shape), hti.reshape(h0i.shape)


def _sb_in_kernel(x_ref, g_ref, wq_ref, wk_ref, wv_ref, wg_ref, q_ref, k_ref, v_ref, gate_ref):
    xn = _rms(x_ref[...], g_ref[...]).astype(BF16)
    for w_ref, o_ref in ((wq_ref, q_ref), (wk_ref, k_ref), (wv_ref, v_ref)):
        p = jnp.dot(xn, w_ref[...], preferred_element_type=F32)
        for h in range(o_ref.shape[0]):
            o_ref[h] = p[:, h * HEAD_DIM:(h + 1) * HEAD_DIM]
    gate_ref[...] = jnp.dot(xn, wg_ref[...], preferred_element_type=F32)


def _sb_in_proj(x, g, wq, wk, wv, wg, tc):
    n, t, d = x.shape
    w = wq.shape[1]
    heads = w // HEAD_DIM
    head_spec = pl.BlockSpec((None, heads, tc, HEAD_DIM), lambda i, c: (i, 0, c, 0))
    head_shape = jax.ShapeDtypeStruct((n, heads, t, HEAD_DIM), F32)
    return pl.pallas_call(
        _sb_in_kernel,
        grid=(n, t // tc),
        in_specs=[pl.BlockSpec((None, tc, d), lambda i, c: (i, c, 0)),
                  _resident((1, d)), _resident((d, w)), _resident((d, w)),
                  _resident((d, w)), _resident((d, w))],
        out_specs=[head_spec, head_spec, head_spec,
                   pl.BlockSpec((None, tc, w), lambda i, c: (i, c, 0))],
        out_shape=[head_shape, head_shape, head_shape, jax.ShapeDtypeStruct((n, t, w), F32)],
        compiler_params=_compiler_params(("parallel", "parallel"), 56),
        name="sb_in_proj",
    )(x, g, wq, wk, wv, wg)


def _strict_lower_ones(size):
    j = lax.broadcasted_iota(jnp.int32, (size, size), 0)
    s = lax.broadcasted_iota(jnp.int32, (size, size), 1)
    return jnp.where(j > s, 1.0, 0.0).astype(BF16)


def _sb_block(q, kb, vb, tri, carry, acc, mask):
    z = lax.dot_general(q, kb, (((1,), (1,)), ((), ())), preferred_element_type=F32)
    sp = jnp.maximum(z, 0.0) + jnp.log(1.0 + jnp.exp(-jnp.abs(z)))
    if mask is not None:
        sp = jnp.where(mask, sp, 0.0)
    hi = sp.astype(BF16)
    lo = (sp - hi.astype(F32)).astype(BF16)
    rest = (jnp.dot(hi, tri, preferred_element_type=F32)
            + jnp.dot(lo, tri, preferred_element_type=F32))
    w = jnp.exp(z - sp - rest - carry)
    if mask is not None:
        w = jnp.where(mask, w, 0.0)
    acc = acc + jnp.dot(w.astype(BF16), vb, preferred_element_type=F32)
    carry = carry + jnp.sum(sp, axis=-1, keepdims=True)
    return carry, acc


def _sb_attn_kernel(*refs, bq, bk, past):
    if past:
        q_ref, kn_ref, vn_ref, kp_ref, vp_ref, o_ref = refs
    else:
        q_ref, kn_ref, vn_ref, o_ref = refs
    qi = pl.program_id(2)
    q = (q_ref[...] * (HEAD_DIM ** -0.5)).astype(BF16)
    tri = _strict_lower_ones(bk)
    tri_diag = tri if bq == bk else _strict_lower_ones(bq)

    start = pl.multiple_of(qi * bq, bq)
    row = lax.broadcasted_iota(jnp.int32, (bq, bq), 0)
    col = lax.broadcasted_iota(jnp.int32, (bq, bq), 1)
    carry = jnp.zeros((bq, 1), F32)
    acc = jnp.zeros((bq, HEAD_DIM), F32)
    carry, acc = _sb_block(q, kn_ref[pl.ds(start, bq), :].astype(BF16),
                           vn_ref[pl.ds(start, bq), :].astype(BF16),
                           tri_diag, carry, acc, col < row)

    def new_block(i, state):
        off = pl.multiple_of((qi - 1 - i) * bk, bk)
        return _sb_block(q, kn_ref[pl.ds(off, bk), :].astype(BF16),
                         vn_ref[pl.ds(off, bk), :].astype(BF16), tri, *state, None)

    if kn_ref.shape[0] > bq:
        carry, acc = lax.fori_loop(0, qi, new_block, (carry, acc))

    if past:
        n_past = past // bk

        def past_block(i, state):
            off = pl.multiple_of((n_past - 1 - i) * bk, bk)
            return _sb_block(q, kp_ref[pl.ds(off, bk), :].astype(BF16),
                             vp_ref[pl.ds(off, bk), :].astype(BF16), tri, *state, None)

        carry, acc = lax.fori_loop(0, n_past, past_block, (carry, acc))

    o_ref[...] = acc


def _sb_attention(q, k, v, k_past, v_past, layer, bq, bk):
    n, heads, t, dh = q.shape
    past = 0 if k_past is None else k_past.shape[3]
    assert t == bq or bq == bk
    assert past % bk == 0
    q_spec = pl.BlockSpec((None, None, bq, dh), lambda i, h, qi: (i, h, qi, 0))
    kv_spec = pl.BlockSpec((None, None, t, dh), lambda i, h, qi: (i, h, 0, 0))
    in_specs = [q_spec, kv_spec, kv_spec]
    args = [q, k, v]
    if past:
        past_spec = pl.BlockSpec((None, None, None, past, dh),
                                 lambda i, h, qi: (layer, i, h, 0, 0))
        in_specs += [past_spec, past_spec]
        args += [k_past, v_past]
    kernel = functools.partial(_sb_attn_kernel, bq=bq, bk=bk, past=past)
    return pl.pallas_call(
        kernel,
        grid=(n, heads, t // bq),
        in_specs=in_specs,
        out_specs=pl.BlockSpec((None, bq, dh), lambda i, h, qi: (i, qi, h)),
        out_shape=jax.ShapeDtypeStruct((n, t, heads * dh), F32),
        compiler_params=_compiler_params(("parallel", "parallel", "parallel"), 32),
        name="sb_attention",
    )(*args)


def _sb_out_kernel(a_ref, gate_ref, x_ref, wout_ref, g_ref, o_ref):
    a = (a_ref[...] * _silu(gate_ref[...])).astype(BF16)
    o = jnp.dot(a, wout_ref[...], preferred_element_type=F32)
    o_ref[...] = x_ref[...] + _rms(o, g_ref[...])


def _sb_out_proj(a, gate, x, wout, g, tc):
    n, t, d = x.shape
    w = gate.shape[2]
    return pl.pallas_call(
        _sb_out_kernel,
        grid=(n, t // tc),
        in_specs=[pl.BlockSpec((None, tc, w), lambda i, c: (i, c, 0)),
                  pl.BlockSpec((None, tc, w), lambda i, c: (i, c, 0)),
                  pl.BlockSpec((None, tc, d), lambda i, c: (i, c, 0)),
                  _resident((w, d)), _resident((1, d))],
        out_specs=pl.BlockSpec((None, tc, d), lambda i, c: (i, c, 0)),
        out_shape=jax.ShapeDtypeStruct((n, t, d), F32),
        compiler_params=_compiler_params(("parallel", "parallel"), 48),
        name="sb_out_proj",
    )(a, gate, x, wout, g)


def _sb_layer(x, k_past, v_past, layer, g_pre, g_post, w_in, w_out, row_tile, bq, bk):
    w = w_out.shape[0]
    wq, wk, wv, wg = (w_in[:, i * w:(i + 1) * w].astype(BF16) for i in range(4))
    q, k, v, gate = _sb_in_proj(x, g_pre[None, :], wq, wk, wv, wg, row_tile)
    a = _sb_attention(q, k, v, k_past, v_past, layer, bq, bk)
    x = _sb_out_proj(a, gate, x, w_out.astype(BF16), g_post[None, :], row_tile)
    return x, k, v


def _run_trunk(x, h0_re, h0_im, k_past, v_past, p, row_tile, scan_tile, bq, bk):
    depth = p["norm_pre"].shape[0]
    new_k, new_v, new_re, new_im = [], [], [], []
    for i in range(depth):
        j = i // 2
        if i % 2 == 0:
            x, hr, hi = _ssm_layer(
                x, h0_re[j], h0_im[j], p["norm_pre"][i], p["norm_post"][i], p["w_in_ssm"][j],
                p["ssm_a_re"][j], p["ssm_a_im"][j], p["ssm_log_step"][j], p["ssm_b_re"][j],
                p["ssm_b_im"][j], p["ssm_c_re"][j], p["ssm_c_im"][j], p["ssm_d"][j],
                p["w_glu"][j], p["w_out_ssm"][j], row_tile, scan_tile)
            new_re.append(hr)
            new_im.append(hi)
        else:
            x, k, v = _sb_layer(x, k_past, v_past, j, p["norm_pre"][i], p["norm_post"][i],
                                p["w_in_sb"][j], p["w_out_sb"][j], row_tile, bq, bk)
            new_k.append(k)
            new_v.append(v)
    return x, jnp.stack(new_k), jnp.stack(new_v), jnp.stack(new_re), jnp.stack(new_im)


def kernel(x_prompt, x_sample, cache_sb_k, cache_sb_v, state_ssm_re, state_ssm_im, norm_pre, norm_post, w_in_ssm, ssm_a_re, ssm_a_im, ssm_log_step, ssm_b_re, ssm_b_im, ssm_c_re, ssm_c_im, ssm_d, w_glu, w_out_ssm, w_in_sb, w_out_sb):
    p = dict(norm_pre=norm_pre, norm_post=norm_post, w_in_ssm=w_in_ssm, ssm_a_re=ssm_a_re,
             ssm_a_im=ssm_a_im, ssm_log_step=ssm_log_step, ssm_b_re=ssm_b_re, ssm_b_im=ssm_b_im,
             ssm_c_re=ssm_c_re, ssm_c_im=ssm_c_im, ssm_d=ssm_d, w_glu=w_glu,
             w_out_ssm=w_out_ssm, w_in_sb=w_in_sb, w_out_sb=w_out_sb)
    nb = x_prompt.shape[0]
    n_ssm = state_ssm_re.shape[0]
    zeros = jnp.zeros((n_ssm, nb) + state_ssm_re.shape[2:], F32)
    y_p, k_p, v_p, re_p, im_p = _run_trunk(x_prompt, zeros, zeros, None, None, p,
                                           row_tile=256, scan_tile=64, bq=256, bk=256)
    t_s = x_sample.shape[1]
    y_s, k_s, v_s, re_s, im_s = _run_trunk(x_sample, state_ssm_re, state_ssm_im,
                                           cache_sb_k, cache_sb_v, p,
                                           row_tile=t_s, scan_tile=t_s, bq=t_s, bk=256)
    return (y_p, y_s, k_p, v_p, re_p, im_p, k_s, v_s, re_s, im_s)
```

```python
import functools
import math

import jax
import jax.numpy as jnp
from jax import lax
from jax.experimental import pallas as pl
from jax.experimental.pallas import tpu as pltpu

F32 = jnp.float32
BF16 = jnp.bfloat16

RMS_EPS = 1e-6
SSM_GROUP = 16
SSM_STATE = 64
HEAD_DIM = 128

V7X_LANES = 128
V7X_VMEM_BYTES = 64 * 1024 * 1024

SB_SKIP_CARRY = 104.0
SB_STATIC_BLOCKS = 2

SSM_CH_TILE = V7X_LANES
SSM_G_TILE = SSM_CH_TILE // SSM_GROUP
SSM_S_TILE = SSM_G_TILE * SSM_STATE
SCAN_COLS = V7X_LANES


def _compiler_params(semantics, vmem_mib):
    assert vmem_mib * 1024 * 1024 < V7X_VMEM_BYTES
    return pltpu.CompilerParams(dimension_semantics=semantics,
                                vmem_limit_bytes=vmem_mib * 1024 * 1024)


def _resident(shape):
    zeros = (0,) * len(shape)
    return pl.BlockSpec(shape, lambda *_: zeros, pipeline_mode=pl.Buffered(1))


def _rms(x, g):
    return x * lax.rsqrt(jnp.mean(x * x, axis=-1, keepdims=True) + RMS_EPS) * g


def _silu(x):
    return x * jax.nn.sigmoid(x)


def _ssm_in_kernel(x_ref, g_ref, wu_ref, wg_ref, u_ref, gate_ref):
    xn = _rms(x_ref[...], g_ref[...]).astype(BF16)
    u_ref[...] = jnp.dot(xn, wu_ref[...], preferred_element_type=F32)
    gate_ref[...] = jnp.dot(xn, wg_ref[...], preferred_element_type=F32)


def _ssm_in_proj(x, g, wu, wg, tc):
    n, t, d = x.shape
    w = wu.shape[1]
    return pl.pallas_call(
        _ssm_in_kernel,
        grid=(n, t // tc),
        in_specs=[pl.BlockSpec((None, tc, d), lambda i, c: (i, c, 0)),
                  _resident((1, d)), _resident((d, w)), _resident((d, w))],
        out_specs=[pl.BlockSpec((tc, w), lambda i, c: (c, i)),
                   pl.BlockSpec((None, tc, w), lambda i, c: (i, c, 0))],
        out_shape=[jax.ShapeDtypeStruct((t, n * w), F32),
                   jax.ShapeDtypeStruct((n, t, w), F32)],
        compiler_params=_compiler_params(("parallel", "parallel"), 48),
        name="ssm_in_proj",
    )(x, g, wu, wg)


def _ssm_mixer_kernel(u_ref, h0r_ref, h0i_ref, bm_ref, cm_ref, lbr_ref, lbi_ref, d_ref,
                      y_ref, htr_ref, hti_ref, sr_ref, si_ref, bu_ref, hs_ref, *, tc, n):
    c = pl.program_id(1)

    @pl.when(c == 0)
    def _():
        sr_ref[...] = h0r_ref[...]
        si_ref[...] = h0i_ref[...]

    u2 = u_ref[...].reshape(tc * n, SSM_CH_TILE)
    bu_ref[...] = jnp.dot(u2.astype(BF16), bm_ref[...], preferred_element_type=F32)

    for q in range(SSM_S_TILE // SCAN_COLS):
        re_cols = slice(q * SCAN_COLS, (q + 1) * SCAN_COLS)
        im_cols = slice(SSM_S_TILE + q * SCAN_COLS, SSM_S_TILE + (q + 1) * SCAN_COLS)
        lr = jnp.broadcast_to(lbr_ref[:, re_cols], (n, SCAN_COLS))
        li = jnp.broadcast_to(lbi_ref[:, re_cols], (n, SCAN_COLS))

        def step(t, carry, re_cols=re_cols, im_cols=im_cols, lr=lr, li=li):
            hr, hi = carry
            rows = pl.ds(pl.multiple_of(t * n, n), n)
            nr = lr * hr - li * hi + bu_ref[rows, re_cols]
            ni = lr * hi + li * hr + bu_ref[rows, im_cols]
            hs_ref[rows, re_cols] = nr.astype(hs_ref.dtype)
            hs_ref[rows, im_cols] = ni.astype(hs_ref.dtype)
            return nr, ni

        hr, hi = lax.fori_loop(0, tc, step, (sr_ref[:, re_cols], si_ref[:, re_cols]),
                               unroll=8)
        sr_ref[:, re_cols] = hr
        si_ref[:, re_cols] = hi

    y = jnp.dot(hs_ref[...].astype(BF16), cm_ref[...], preferred_element_type=F32)
    y = y + d_ref[...] * u2
    y_ref[...] = y.reshape(tc, n, SSM_CH_TILE)

    @pl.when(c == pl.num_programs(1) - 1)
    def _():
        htr_ref[...] = sr_ref[...]
        hti_ref[...] = si_ref[...]


def _ssm_mixer(u_tm, h0r, h0i, bm, cm, lbr, lbi, dskip, tc):
    t, n, w = u_tm.shape
    tiles = w // SSM_CH_TILE
    hs_dtype = BF16 if n % 16 == 0 else F32
    kernel = functools.partial(_ssm_mixer_kernel, tc=tc, n=n)
    return pl.pallas_call(
        kernel,
        grid=(tiles, t // tc),
        in_specs=[pl.BlockSpec((tc, n, SSM_CH_TILE), lambda j, c: (c, 0, j)),
                  pl.BlockSpec((n, SSM_S_TILE), lambda j, c: (0, j)),
                  pl.BlockSpec((n, SSM_S_TILE), lambda j, c: (0, j)),
                  pl.BlockSpec((None, SSM_CH_TILE, 2 * SSM_S_TILE), lambda j, c: (j, 0, 0)),
                  pl.BlockSpec((None, 2 * SSM_S_TILE, SSM_CH_TILE), lambda j, c: (j, 0, 0)),
                  pl.BlockSpec((None, 1, SSM_S_TILE), lambda j, c: (j, 0, 0)),
                  pl.BlockSpec((None, 1, SSM_S_TILE), lambda j, c: (j, 0, 0)),
                  pl.BlockSpec((1, SSM_CH_TILE), lambda j, c: (0, j))],
        out_specs=[pl.BlockSpec((tc, n, SSM_CH_TILE), lambda j, c: (c, 0, j)),
                   pl.BlockSpec((n, SSM_S_TILE), lambda j, c: (0, j)),
                   pl.BlockSpec((n, SSM_S_TILE), lambda j, c: (0, j))],
        out_shape=[jax.ShapeDtypeStruct((t, n, w), F32),
                   jax.ShapeDtypeStruct(h0r.shape, F32),
                   jax.ShapeDtypeStruct(h0i.shape, F32)],
        scratch_shapes=[pltpu.VMEM((n, SSM_S_TILE), F32),
                        pltpu.VMEM((n, SSM_S_TILE), F32),
                        pltpu.VMEM((tc * n, 2 * SSM_S_TILE), F32),
                        pltpu.VMEM((tc * n, 2 * SSM_S_TILE), hs_dtype)],
        compiler_params=_compiler_params(("parallel", "arbitrary"), 48),
        name="ssm_mixer",
    )(u_tm, h0r, h0i, bm, cm, lbr, lbi, dskip)


def _ssm_out_kernel(y_ref, gate_ref, x_ref, wglu_ref, wout_ref, g_ref, o_ref):
    y = jax.nn.gelu(y_ref[...])
    z = jnp.dot(y.astype(BF16), wglu_ref[...], preferred_element_type=F32)
    y = y * jax.nn.sigmoid(z)
    a = (y * _silu(gate_ref[...])).astype(BF16)
    o = jnp.dot(a, wout_ref[...], preferred_element_type=F32)
    o_ref[...] = x_ref[...] + _rms(o, g_ref[...])


def _ssm_out_proj(y_tm2, gate, x, wglu, wout, g, tc):
    n, t, d = x.shape
    w = gate.shape[2]
    return pl.pallas_call(
        _ssm_out_kernel,
        grid=(n, t // tc),
        in_specs=[pl.BlockSpec((tc, w), lambda i, c: (c, i)),
                  pl.BlockSpec((None, tc, w), lambda i, c: (i, c, 0)),
                  pl.BlockSpec((None, tc, d), lambda i, c: (i, c, 0)),
                  _resident((w, w)), _resident((w, d)), _resident((1, d))],
        out_specs=pl.BlockSpec((None, tc, d), lambda i, c: (i, c, 0)),
        out_shape=jax.ShapeDtypeStruct((n, t, d), F32),
        compiler_params=_compiler_params(("parallel", "parallel"), 48),
        name="ssm_out_proj",
    )(y_tm2, gate, x, wglu, wout, g)


def _ssm_weights(a_re, a_im, log_step, b_re, b_im, c_re, c_im):
    ar = a_re.astype(F32)
    ai = a_im.astype(F32)
    dt = jnp.exp(log_step.astype(F32))[:, None]
    mag = jnp.exp(ar * dt)
    ph = ai * dt
    lb_re = mag * jnp.cos(ph)
    lb_im = mag * jnp.sin(ph)
    nr = lb_re - 1.0
    ni = lb_im
    den = ar * ar + ai * ai
    fac_re = (nr * ar + ni * ai) / den
    fac_im = (ni * ar - nr * ai) / den
    br = b_re.astype(F32)
    bi = b_im.astype(F32)
    bbar_re = fac_re[..., None] * br - fac_im[..., None] * bi
    bbar_im = fac_re[..., None] * bi + fac_im[..., None] * br

    groups = a_re.shape[0]
    tiles = groups // SSM_G_TILE
    eye = jnp.eye(SSM_G_TILE, dtype=F32)

    def b_tile(b):
        b = b.reshape(tiles, SSM_G_TILE, SSM_STATE, SSM_GROUP).transpose(0, 1, 3, 2)
        blk = b[:, :, :, None, :] * eye[None, :, None, :, None]
        return blk.reshape(tiles, SSM_CH_TILE, SSM_S_TILE)

    def c_tile(cc):
        cc = cc.reshape(tiles, SSM_G_TILE, SSM_GROUP, SSM_STATE).transpose(0, 1, 3, 2)
        blk = cc[:, :, :, None, :] * eye[None, :, None, :, None]
        return blk.reshape(tiles, SSM_S_TILE, SSM_CH_TILE)

    bm = jnp.concatenate([b_tile(bbar_re), b_tile(bbar_im)], axis=2).astype(BF16)
    cm = jnp.concatenate([c_tile(c_re.astype(F32)), -c_tile(c_im.astype(F32))], axis=1).astype(BF16)
    lbr = lb_re.reshape(tiles, 1, SSM_S_TILE)
    lbi = lb_im.reshape(tiles, 1, SSM_S_TILE)
    return bm, cm, lbr, lbi


def _ssm_layer(x, h0r, h0i, g_pre, g_post, w_in, a_re, a_im, log_step, b_re, b_im,
               c_re, c_im, d_skip, w_glu, w_out, row_tile, scan_tile):
    n, t, _ = x.shape
    w = w_glu.shape[0]
    wu = w_in[:, :w].astype(BF16)
    wg = w_in[:, w:].astype(BF16)
    bm, cm, lbr, lbi = _ssm_weights(a_re, a_im, log_step, b_re, b_im, c_re, c_im)
    u_tm2, gate = _ssm_in_proj(x, g_pre[None, :], wu, wg, row_tile)
    y_tm, htr, hti = _ssm_mixer(u_tm2.reshape(t, n, w), h0r.reshape(n, -1), h0i.reshape(n, -1),
                                bm, cm, lbr, lbi, d_skip[None, :], scan_tile)
    x = _ssm_out_proj(y_tm.reshape(t, n * w), gate, x, w_glu.astype(BF16), w_out.astype(BF16),
                      g_post[None, :], row_tile)
    return x, htr.reshape(h0r.shape), hti.reshape(h0i.shape)


def _sb_in_kernel(x_ref, g_ref, wq_ref, wk_ref, wv_ref, wg_ref, *rest):
    q_ref, k_ref, v_ref, gate_ref = rest[-4:]
    xn = _rms(x_ref[...], g_ref[...]).astype(BF16)
    for w_ref, o_ref in ((wq_ref, q_ref), (wk_ref, k_ref), (wv_ref, v_ref)):
        p = jnp.dot(xn, w_ref[...], preferred_element_type=F32)
        for h in range(o_ref.shape[0]):
            o_ref[h] = p[:, h * HEAD_DIM:(h + 1) * HEAD_DIM]
    gate_ref[...] = jnp.dot(xn, wg_ref[...], preferred_element_type=F32)


def _sb_in_proj(x, g, wq, wk, wv, wg, tc, layer, n_layers, kv_stacks):
    n, t, d = x.shape
    w = wq.shape[1]
    heads = w // HEAD_DIM
    q_spec = pl.BlockSpec((None, heads, tc, HEAD_DIM), lambda i, c: (i, 0, c, 0))
    kv_spec = pl.BlockSpec((None, None, heads, tc, HEAD_DIM), lambda i, c: (layer, i, 0, c, 0))
    kv_shape = jax.ShapeDtypeStruct((n_layers, n, heads, t, HEAD_DIM), F32)
    in_specs = [pl.BlockSpec((None, tc, d), lambda i, c: (i, c, 0)),
                _resident((1, d)), _resident((d, w)), _resident((d, w)),
                _resident((d, w)), _resident((d, w))]
    args = [x, g, wq, wk, wv, wg]
    aliases = {}
    if kv_stacks is not None:
        aliases = {len(args): 1, len(args) + 1: 2}
        in_specs += [pl.BlockSpec(memory_space=pl.ANY)] * 2
        args += list(kv_stacks)
    return pl.pallas_call(
        _sb_in_kernel,
        grid=(n, t // tc),
        in_specs=in_specs,
        out_specs=[q_spec, kv_spec, kv_spec, pl.BlockSpec((None, tc, w), lambda i, c: (i, c, 0))],
        out_shape=[jax.ShapeDtypeStruct((n, heads, t, HEAD_DIM), F32), kv_shape, kv_shape,
                   jax.ShapeDtypeStruct((n, t, w), F32)],
        input_output_aliases=aliases,
        compiler_params=_compiler_params(("parallel", "parallel"), 56),
        name="sb_in_proj",
    )(*args)


def _strict_lower_ones(size):
    j = lax.broadcasted_iota(jnp.int32, (size, size), 0)
    s = lax.broadcasted_iota(jnp.int32, (size, size), 1)
    return jnp.where(j > s, 1.0, 0.0).astype(BF16)


def _sb_tile(q, kb, vb, tri, carry, mask):
    z = lax.dot_general(q, kb, (((1,), (1,)), ((), ())), preferred_element_type=F32)
    sp = jnp.maximum(z, 0.0) + jnp.log(1.0 + jnp.exp(-jnp.abs(z)))
    if mask is not None:
        sp = jnp.where(mask, sp, 0.0)
    hi = sp.astype(BF16)
    lo = (sp - hi.astype(F32)).astype(BF16)
    rest = (jnp.dot(hi, tri, preferred_element_type=F32)
            + jnp.dot(lo, tri, preferred_element_type=F32))
    arg = z - sp - rest
    total = jnp.sum(sp, axis=-1, keepdims=True)
    if carry is not None:
        arg = arg - carry
        total = total + carry
    w = jnp.exp(arg)
    if mask is not None:
        w = jnp.where(mask, w, 0.0)
    return total, jnp.dot(w.astype(BF16), vb, preferred_element_type=F32)


def _sb_tail(q, carry, rows, k_ref, v_ref, n_blocks, blk, tri, o_ref):
    def body(j, carry):
        off = pl.multiple_of((n_blocks - 1 - j) * blk, blk)
        kb = k_ref[pl.ds(off, blk), :].astype(BF16)
        vb = v_ref[pl.ds(off, blk), :].astype(BF16)
        carry, part = _sb_tile(q, kb, vb, tri, carry, None)
        o_ref[rows, :] += part
        return carry

    return lax.fori_loop(0, n_blocks, body, carry)


def _sb_attn_kernel(*refs, bq, bk, past):
    if past:
        q_ref, kn_ref, vn_ref, kp_ref, vp_ref, o_ref = refs
    else:
        q_ref, kn_ref, vn_ref, o_ref = refs
    t = q_ref.shape[0]
    nq = t // bq
    n_past = past // bk
    tri_new = _strict_lower_ones(bq)
    tri_past = tri_new if bk == bq else _strict_lower_ones(bk)
    row = lax.broadcasted_iota(jnp.int32, (bq, bq), 0)
    col = lax.broadcasted_iota(jnp.int32, (bq, bq), 1)
    earlier = col < row

    casts = {}

    def kv_block(k_ref, v_ref, b, blk):
        key = (id(k_ref), b)
        if key not in casts:
            rows = slice(b * blk, (b + 1) * blk)
            casts[key] = (k_ref[rows, :].astype(BF16), v_ref[rows, :].astype(BF16))
        return casts[key]

    tails = []
    for i in range(nq):
        rows = slice(i * bq, (i + 1) * bq)
        q = (q_ref[rows, :] * (HEAD_DIM ** -0.5)).astype(BF16)
        new_left, past_left = i + 1, n_past
        carry = acc = None
        for _ in range(SB_STATIC_BLOCKS):
            if new_left:
                new_left -= 1
                kb, vb = kv_block(kn_ref, vn_ref, new_left, bq)
                carry, part = _sb_tile(q, kb, vb, tri_new, carry, earlier if new_left == i else None)
            elif past_left:
                past_left -= 1
                kb, vb = kv_block(kp_ref, vp_ref, past_left, bk)
                carry, part = _sb_tile(q, kb, vb, tri_past, carry, None)
            else:
                break
            acc = part if acc is None else acc + part
        o_ref[rows, :] = acc
        if new_left or past_left:
            tails.append((rows, q, carry, new_left, past_left))

    if tails:
        lowest = functools.reduce(jnp.minimum, [c for _, _, c, _, _ in tails])

        @pl.when(jnp.min(lowest) < SB_SKIP_CARRY)
        def _():
            for rows, q, carry, new_left, past_left in tails:
                @pl.when(jnp.min(carry) < SB_SKIP_CARRY)
                def _(rows=rows, q=q, carry=carry, new_left=new_left, past_left=past_left):
                    if new_left:
                        carry = _sb_tail(q, carry, rows, kn_ref, vn_ref, new_left, bq, tri_new, o_ref)
                    if past_left:
                        _sb_tail(q, carry, rows, kp_ref, vp_ref, past_left, bk, tri_past, o_ref)


def _sb_attention(q, k, v, k_past, v_past, layer, bq, bk):
    n, heads, t, dh = q.shape
    past = 0 if k_past is None else k_past.shape[3]
    assert t % bq == 0 and past % bk == 0
    q_spec = pl.BlockSpec((None, None, t, dh), lambda i, h: (i, h, 0, 0))
    kv_spec = pl.BlockSpec((None, None, None, t, dh), lambda i, h: (layer, i, h, 0, 0))
    in_specs = [q_spec, kv_spec, kv_spec]
    args = [q, k, v]
    if past:
        past_spec = pl.BlockSpec((None, None, None, past, dh), lambda i, h: (layer, i, h, 0, 0))
        in_specs += [past_spec, past_spec]
        args += [k_past, v_past]
    kernel = functools.partial(_sb_attn_kernel, bq=bq, bk=bk, past=past)
    return pl.pallas_call(
        kernel,
        grid=(n, heads),
        in_specs=in_specs,
        out_specs=pl.BlockSpec((None, t, dh), lambda i, h: (i, 0, h)),
        out_shape=jax.ShapeDtypeStruct((n, t, heads * dh), F32),
        compiler_params=_compiler_params(("parallel", "parallel"), 48),
        name="sb_attention",
    )(*args)


def _sb_out_kernel(a_ref, gate_ref, x_ref, wout_ref, g_ref, o_ref):
    a = (a_ref[...] * _silu(gate_ref[...])).astype(BF16)
    o = jnp.dot(a, wout_ref[...], preferred_element_type=F32)
    o_ref[...] = x_ref[...] + _rms(o, g_ref[...])


def _sb_out_proj(a, gate, x, wout, g, tc):
    n, t, d = x.shape
    w = gate.shape[2]
    return pl.pallas_call(
        _sb_out_kernel,
        grid=(n, t // tc),
        in_specs=[pl.BlockSpec((None, tc, w), lambda i, c: (i, c, 0)),
                  pl.BlockSpec((None, tc, w), lambda i, c: (i, c, 0)),
                  pl.BlockSpec((None, tc, d), lambda i, c: (i, c, 0)),
                  _resident((w, d)), _resident((1, d))],
        out_specs=pl.BlockSpec((None, tc, d), lambda i, c: (i, c, 0)),
        out_shape=jax.ShapeDtypeStruct((n, t, d), F32),
        compiler_params=_compiler_params(("parallel", "parallel"), 48),
        name="sb_out_proj",
    )(a, gate, x, wout, g)


def _sb_layer(x, k_past, v_past, kv_stacks, layer, n_layers, g_pre, g_post, w_in, w_out,
              row_tile, bq, bk):
    w = w_out.shape[0]
    wq, wk, wv, wg = (w_in[:, i * w:(i + 1) * w].astype(BF16) for i in range(4))
    q, k, v, gate = _sb_in_proj(x, g_pre[None, :], wq, wk, wv, wg, row_tile, layer, n_layers,
                                kv_stacks)
    a = _sb_attention(q, k, v, k_past, v_past, layer, bq, bk)
    x = _sb_out_proj(a, gate, x, w_out.astype(BF16), g_post[None, :], row_tile)
    return x, (k, v)


def _run_trunk(x, h0_re, h0_im, k_past, v_past, p, row_tile, scan_tile, bq, bk):
    depth = p["norm_pre"].shape[0]
    n_sb = p["w_in_sb"].shape[0]
    kv_stacks = None
    new_re, new_im = [], []
    for i in range(depth):
        j = i // 2
        if i % 2 == 0:
            x, hr, hi = _ssm_layer(
                x, h0_re[j], h0_im[j], p["norm_pre"][i], p["norm_post"][i], p["w_in_ssm"][j],
                p["ssm_a_re"][j], p["ssm_a_im"][j], p["ssm_log_step"][j], p["ssm_b_re"][j],
                p["ssm_b_im"][j], p["ssm_c_re"][j], p["ssm_c_im"][j], p["ssm_d"][j],
                p["w_glu"][j], p["w_out_ssm"][j], row_tile, scan_tile)
            new_re.append(hr)
            new_im.append(hi)
        else:
            x, kv_stacks = _sb_layer(x, k_past, v_past, kv_stacks, j, n_sb, p["norm_pre"][i],
                                     p["norm_post"][i], p["w_in_sb"][j], p["w_out_sb"][j],
                                     row_tile, bq, bk)
    return x, kv_stacks[0], kv_stacks[1], jnp.stack(new_re), jnp.stack(new_im)


def kernel(x_prompt, x_sample, cache_sb_k, cache_sb_v, state_ssm_re, state_ssm_im, norm_pre, norm_post, w_in_ssm, ssm_a_re, ssm_a_im, ssm_log_step, ssm_b_re, ssm_b_im, ssm_c_re, ssm_c_im, ssm_d, w_glu, w_out_ssm, w_in_sb, w_out_sb):
    p = dict(norm_pre=norm_pre, norm_post=norm_post, w_in_ssm=w_in_ssm, ssm_a_re=ssm_a_re,
             ssm_a_im=ssm_a_im, ssm_log_step=ssm_log_step, ssm_b_re=ssm_b_re, ssm_b_im=ssm_b_im,
             ssm_c_re=ssm_c_re, ssm_c_im=ssm_c_im, ssm_d=ssm_d, w_glu=w_glu,
             w_out_ssm=w_out_ssm, w_in_sb=w_in_sb, w_out_sb=w_out_sb)
    nb = x_prompt.shape[0]
    n_ssm = state_ssm_re.shape[0]
    zeros = jnp.zeros((n_ssm, nb) + state_ssm_re.shape[2:], F32)
    y_p, k_p, v_p, re_p, im_p = _run_trunk(x_prompt, zeros, zeros, None, None, p,
                                           row_tile=256, scan_tile=64, bq=256, bk=256)
    t_s = x_sample.shape[1]
    y_s, k_s, v_s, re_s, im_s = _run_trunk(x_sample, state_ssm_re, state_ssm_im,
                                           cache_sb_k, cache_sb_v, p,
                                           row_tile=t_s, scan_tile=t_s, bq=t_s, bk=256)
    return (y_p, y_s, k_p, v_p, re_p, im_p, k_s, v_s, re_s, im_s)
```

```python
import functools
import math

import jax
import jax.numpy as jnp
from jax import lax
from jax.experimental import pallas as pl
from jax.experimental.pallas import tpu as pltpu

F32 = jnp.float32
BF16 = jnp.bfloat16

RMS_EPS = 1e-6
SSM_GROUP = 16
SSM_STATE = 64
HEAD_DIM = 128

V7X_LANES = 128
V7X_VMEM_BYTES = 64 * 1024 * 1024

LOG2_E = math.log2(math.e)
SB_SKIP_CARRY = 104.0 * LOG2_E
SB_STATIC_BLOCKS = 2

SSM_CH_TILE = V7X_LANES
SSM_G_TILE = SSM_CH_TILE // SSM_GROUP
SSM_S_TILE = SSM_G_TILE * SSM_STATE
SCAN_COLS = V7X_LANES


def _compiler_params(semantics, vmem_mib):
    assert vmem_mib * 1024 * 1024 < V7X_VMEM_BYTES
    return pltpu.CompilerParams(dimension_semantics=semantics,
                                vmem_limit_bytes=vmem_mib * 1024 * 1024)


def _resident(shape):
    zeros = (0,) * len(shape)
    return pl.BlockSpec(shape, lambda *_: zeros, pipeline_mode=pl.Buffered(1))


def _rms(x, g):
    return x * lax.rsqrt(jnp.mean(x * x, axis=-1, keepdims=True) + RMS_EPS) * g


def _silu(x):
    return x * jax.nn.sigmoid(x)


def _ssm_in_kernel(x_ref, g_ref, wu_ref, wg_ref, u_ref, gate_ref):
    xn = _rms(x_ref[...], g_ref[...]).astype(BF16)
    u_ref[...] = jnp.dot(xn, wu_ref[...], preferred_element_type=F32)
    gate_ref[...] = jnp.dot(xn, wg_ref[...], preferred_element_type=F32)


def _ssm_in_proj(x, g, wu, wg, tc):
    n, t, d = x.shape
    w = wu.shape[1]
    return pl.pallas_call(
        _ssm_in_kernel,
        grid=(n, t // tc),
        in_specs=[pl.BlockSpec((None, tc, d), lambda i, c: (i, c, 0)),
                  _resident((1, d)), _resident((d, w)), _resident((d, w))],
        out_specs=[pl.BlockSpec((tc, w), lambda i, c: (c, i)),
                   pl.BlockSpec((None, tc, w), lambda i, c: (i, c, 0))],
        out_shape=[jax.ShapeDtypeStruct((t, n * w), F32),
                   jax.ShapeDtypeStruct((n, t, w), F32)],
        compiler_params=_compiler_params(("parallel", "parallel"), 48),
        name="ssm_in_proj",
    )(x, g, wu, wg)


def _ssm_mixer_kernel(u_ref, h0r_ref, h0i_ref, bm_ref, cm_ref, lbr_ref, lbi_ref, d_ref,
                      y_ref, htr_ref, hti_ref, sr_ref, si_ref, bu_ref, hs_ref, *, tc, n):
    c = pl.program_id(1)

    @pl.when(c == 0)
    def _():
        sr_ref[...] = h0r_ref[...]
        si_ref[...] = h0i_ref[...]

    u2 = u_ref[...].reshape(tc * n, SSM_CH_TILE)
    u2b = u2.astype(BF16)
    y = d_ref[...] * u2

    for q in range(SSM_S_TILE // SCAN_COLS):
        st_cols = slice(q * SCAN_COLS, (q + 1) * SCAN_COLS)
        re_cols = slice(2 * q * SCAN_COLS, (2 * q + 1) * SCAN_COLS)
        im_cols = slice((2 * q + 1) * SCAN_COLS, (2 * q + 2) * SCAN_COLS)
        both = slice(2 * q * SCAN_COLS, (2 * q + 2) * SCAN_COLS)
        bu_ref[:, both] = jnp.dot(u2b, bm_ref[:, both], preferred_element_type=F32)
        lr = jnp.broadcast_to(lbr_ref[:, st_cols], (n, SCAN_COLS))
        li = jnp.broadcast_to(lbi_ref[:, st_cols], (n, SCAN_COLS))
        hr = sr_ref[:, st_cols]
        hi = si_ref[:, st_cols]
        for t in range(tc):
            rows = slice(t * n, (t + 1) * n)
            hr, hi = (lr * hr - li * hi + bu_ref[rows, re_cols],
                      lr * hi + li * hr + bu_ref[rows, im_cols])
            hs_ref[rows, re_cols] = hr.astype(hs_ref.dtype)
            hs_ref[rows, im_cols] = hi.astype(hs_ref.dtype)
        sr_ref[:, st_cols] = hr
        si_ref[:, st_cols] = hi
        y = y + jnp.dot(hs_ref[:, both].astype(BF16), cm_ref[both, :],
                        preferred_element_type=F32)

    y_ref[...] = y.reshape(tc, n, SSM_CH_TILE)

    @pl.when(c == pl.num_programs(1) - 1)
    def _():
        htr_ref[...] = sr_ref[...]
        hti_ref[...] = si_ref[...]


def _ssm_mixer(u_tm, h0r, h0i, bm, cm, lbr, lbi, dskip, tc):
    t, n, w = u_tm.shape
    tiles = w // SSM_CH_TILE
    hs_dtype = BF16 if n % 16 == 0 else F32
    kernel = functools.partial(_ssm_mixer_kernel, tc=tc, n=n)
    return pl.pallas_call(
        kernel,
        grid=(tiles, t // tc),
        in_specs=[pl.BlockSpec((tc, n, SSM_CH_TILE), lambda j, c: (c, 0, j)),
                  pl.BlockSpec((n, SSM_S_TILE), lambda j, c: (0, j)),
                  pl.BlockSpec((n, SSM_S_TILE), lambda j, c: (0, j)),
                  pl.BlockSpec((None, SSM_CH_TILE, 2 * SSM_S_TILE), lambda j, c: (j, 0, 0)),
                  pl.BlockSpec((None, 2 * SSM_S_TILE, SSM_CH_TILE), lambda j, c: (j, 0, 0)),
                  pl.BlockSpec((None, 1, SSM_S_TILE), lambda j, c: (j, 0, 0)),
                  pl.BlockSpec((None, 1, SSM_S_TILE), lambda j, c: (j, 0, 0)),
                  pl.BlockSpec((1, SSM_CH_TILE), lambda j, c: (0, j))],
        out_specs=[pl.BlockSpec((tc, n, SSM_CH_TILE), lambda j, c: (c, 0, j)),
                   pl.BlockSpec((n, SSM_S_TILE), lambda j, c: (0, j)),
                   pl.BlockSpec((n, SSM_S_TILE), lambda j, c: (0, j))],
        out_shape=[jax.ShapeDtypeStruct((t, n, w), F32),
                   jax.ShapeDtypeStruct(h0r.shape, F32),
                   jax.ShapeDtypeStruct(h0i.shape, F32)],
        scratch_shapes=[pltpu.VMEM((n, SSM_S_TILE), F32),
                        pltpu.VMEM((n, SSM_S_TILE), F32),
                        pltpu.VMEM((tc * n, 2 * SSM_S_TILE), F32),
                        pltpu.VMEM((tc * n, 2 * SSM_S_TILE), hs_dtype)],
        compiler_params=_compiler_params(("parallel", "arbitrary"), 48),
        name="ssm_mixer",
    )(u_tm, h0r, h0i, bm, cm, lbr, lbi, dskip)


def _ssm_out_kernel(y_ref, gate_ref, x_ref, wglu_ref, wout_ref, g_ref, o_ref):
    y = jax.nn.gelu(y_ref[...])
    z = jnp.dot(y.astype(BF16), wglu_ref[...], preferred_element_type=F32)
    y = y * jax.nn.sigmoid(z)
    a = (y * _silu(gate_ref[...])).astype(BF16)
    o = jnp.dot(a, wout_ref[...], preferred_element_type=F32)
    o_ref[...] = x_ref[...] + _rms(o, g_ref[...])


def _ssm_out_proj(y_tm, gate, x, wglu, wout, g, tc):
    n, t, d = x.shape
    w = gate.shape[2]
    return pl.pallas_call(
        _ssm_out_kernel,
        grid=(n, t // tc),
        in_specs=[pl.BlockSpec((tc, w), lambda i, c: (c, i)),
                  pl.BlockSpec((None, tc, w), lambda i, c: (i, c, 0)),
                  pl.BlockSpec((None, tc, d), lambda i, c: (i, c, 0)),
                  _resident((w, w)), _resident((w, d)), _resident((1, d))],
        out_specs=pl.BlockSpec((None, tc, d), lambda i, c: (i, c, 0)),
        out_shape=jax.ShapeDtypeStruct((n, t, d), F32),
        compiler_params=_compiler_params(("parallel", "parallel"), 48),
        name="ssm_out_proj",
    )(y_tm, gate, x, wglu, wout, g)


def _ssm_weights(a_re, a_im, log_step, b_re, b_im, c_re, c_im):
    ar = a_re.astype(F32)
    ai = a_im.astype(F32)
    dt = jnp.exp(log_step.astype(F32))[:, None]
    mag = jnp.exp(ar * dt)
    ph = ai * dt
    lb_re = mag * jnp.cos(ph)
    lb_im = mag * jnp.sin(ph)
    nr = lb_re - 1.0
    ni = lb_im
    den = ar * ar + ai * ai
    fac_re = (nr * ar + ni * ai) / den
    fac_im = (ni * ar - nr * ai) / den
    br = b_re.astype(F32)
    bi = b_im.astype(F32)
    bbar_re = fac_re[..., None] * br - fac_im[..., None] * bi
    bbar_im = fac_re[..., None] * bi + fac_im[..., None] * br

    groups = a_re.shape[0]
    tiles = groups // SSM_G_TILE
    eye = jnp.eye(SSM_G_TILE, dtype=F32)

    def b_tile(b):
        b = b.reshape(tiles, SSM_G_TILE, SSM_STATE, SSM_GROUP).transpose(0, 1, 3, 2)
        blk = b[:, :, :, None, :] * eye[None, :, None, :, None]
        return blk.reshape(tiles, SSM_CH_TILE, SSM_S_TILE)

    def c_tile(cc):
        cc = cc.reshape(tiles, SSM_G_TILE, SSM_GROUP, SSM_STATE).transpose(0, 1, 3, 2)
        blk = cc[:, :, :, None, :] * eye[None, :, None, :, None]
        return blk.reshape(tiles, SSM_S_TILE, SSM_CH_TILE)

    passes = SSM_S_TILE // SCAN_COLS

    def interleave(re, im, axis):
        shape = list(re.shape)
        split = shape[:axis] + [passes, SCAN_COLS] + shape[axis + 1:]
        both = jnp.stack([re.reshape(split), im.reshape(split)], axis=axis + 1)
        shape[axis] *= 2
        return both.reshape(shape)

    bm = interleave(b_tile(bbar_re), b_tile(bbar_im), 2).astype(BF16)
    cm = interleave(c_tile(c_re.astype(F32)), -c_tile(c_im.astype(F32)), 1).astype(BF16)
    lbr = lb_re.reshape(tiles, 1, SSM_S_TILE)
    lbi = lb_im.reshape(tiles, 1, SSM_S_TILE)
    return bm, cm, lbr, lbi


def _ssm_layer(x, h0r, h0i, g_pre, g_post, w_in, a_re, a_im, log_step, b_re, b_im,
               c_re, c_im, d_skip, w_glu, w_out, row_tile, scan_tile):
    n, t, _ = x.shape
    w = w_glu.shape[0]
    wu = w_in[:, :w].astype(BF16)
    wg = w_in[:, w:].astype(BF16)
    bm, cm, lbr, lbi = _ssm_weights(a_re, a_im, log_step, b_re, b_im, c_re, c_im)
    u_tm, gate = _ssm_in_proj(x, g_pre[None, :], wu, wg, row_tile)
    y_tm, htr, hti = _ssm_mixer(u_tm.reshape(t, n, w), h0r.reshape(n, -1), h0i.reshape(n, -1),
                                bm, cm, lbr, lbi, d_skip[None, :], scan_tile)
    x = _ssm_out_proj(y_tm.reshape(t, n * w), gate, x, w_glu.astype(BF16), w_out.astype(BF16),
                      g_post[None, :], row_tile)
    return x, htr.reshape(h0r.shape), hti.reshape(h0i.shape)


def _sb_in_kernel(x_ref, g_ref, wq_ref, wk_ref, wv_ref, wg_ref, *rest):
    q_ref, k_ref, v_ref, gate_ref = rest[-4:]
    xn = _rms(x_ref[...], g_ref[...]).astype(BF16)
    for w_ref, o_ref in ((wq_ref, q_ref), (wk_ref, k_ref), (wv_ref, v_ref)):
        p = jnp.dot(xn, w_ref[...], preferred_element_type=F32)
        for h in range(o_ref.shape[0]):
            o_ref[h] = p[:, h * HEAD_DIM:(h + 1) * HEAD_DIM]
    gate_ref[...] = jnp.dot(xn, wg_ref[...], preferred_element_type=F32)


def _sb_in_proj(x, g, wq, wk, wv, wg, tc, layer, n_layers, kv_stacks):
    n, t, d = x.shape
    w = wq.shape[1]
    heads = w // HEAD_DIM
    q_spec = pl.BlockSpec((None, heads, tc, HEAD_DIM), lambda i, c: (i, 0, c, 0))
    kv_spec = pl.BlockSpec((None, None, heads, tc, HEAD_DIM), lambda i, c: (layer, i, 0, c, 0))
    kv_shape = jax.ShapeDtypeStruct((n_layers, n, heads, t, HEAD_DIM), F32)
    in_specs = [pl.BlockSpec((None, tc, d), lambda i, c: (i, c, 0)),
                _resident((1, d)), _resident((d, w)), _resident((d, w)),
                _resident((d, w)), _resident((d, w))]
    args = [x, g, wq, wk, wv, wg]
    aliases = {}
    if kv_stacks is not None:
        aliases = {len(args): 1, len(args) + 1: 2}
        in_specs += [pl.BlockSpec(memory_space=pl.ANY)] * 2
        args += list(kv_stacks)
    return pl.pallas_call(
        _sb_in_kernel,
        grid=(n, t // tc),
        in_specs=in_specs,
        out_specs=[q_spec, kv_spec, kv_spec, pl.BlockSpec((None, tc, w), lambda i, c: (i, c, 0))],
        out_shape=[jax.ShapeDtypeStruct((n, heads, t, HEAD_DIM), F32), kv_shape, kv_shape,
                   jax.ShapeDtypeStruct((n, t, w), F32)],
        input_output_aliases=aliases,
        compiler_params=_compiler_params(("parallel", "parallel"), 56),
        name="sb_in_proj",
    )(*args)


def _strict_lower_ones(size):
    j = lax.broadcasted_iota(jnp.int32, (2 * size, size), 0)
    s = lax.broadcasted_iota(jnp.int32, (2 * size, size), 1)
    return jnp.where(jnp.where(j >= size, j - size, j) > s, 1.0, 0.0).astype(BF16)


def _sb_tile(q, kb, vb, tri, carry, mask):
    z = lax.dot_general(q, kb, (((1,), (1,)), ((), ())), preferred_element_type=F32)
    neg_abs = lax.bitcast_convert_type(
        lax.bitcast_convert_type(z, jnp.uint32) | jnp.uint32(0x80000000), F32)
    sp = jnp.maximum(z, 0.0) + jnp.log(1.0 + jnp.exp2(neg_abs)) * LOG2_E
    if mask is not None:
        sp = jnp.where(mask, sp, 0.0)
    hi = sp.astype(BF16)
    lo = (sp - hi.astype(F32)).astype(BF16)
    rest = jnp.dot(jnp.concatenate([hi, lo], axis=1), tri, preferred_element_type=F32)
    arg = z - sp - rest
    total = jnp.sum(sp, axis=-1, keepdims=True)
    if carry is not None:
        arg = arg - carry
        total = total + carry
    w = jnp.exp2(arg)
    if mask is not None:
        w = jnp.where(mask, w, 0.0)
    return total, jnp.dot(w.astype(BF16), vb, preferred_element_type=F32)


def _sb_tail(q, carry, rows, k_ref, v_ref, n_blocks, blk, tri, o_ref):
    def body(j, carry):
        off = pl.multiple_of((n_blocks - 1 - j) * blk, blk)
        kb = k_ref[pl.ds(off, blk), :].astype(BF16)
        vb = v_ref[pl.ds(off, blk), :].astype(BF16)
        carry, part = _sb_tile(q, kb, vb, tri, carry, None)
        o_ref[rows, :] += part
        return carry

    return lax.fori_loop(0, n_blocks, body, carry)


def _sb_attn_kernel(*refs, bq, bk, past):
    if past:
        q_ref, kn_ref, vn_ref, kp_ref, vp_ref, o_ref = refs
    else:
        q_ref, kn_ref, vn_ref, o_ref = refs
    t = q_ref.shape[0]
    nq = t // bq
    n_past = past // bk
    tri_new = _strict_lower_ones(bq)
    tri_past = tri_new if bk == bq else _strict_lower_ones(bk)
    row = lax.broadcasted_iota(jnp.int32, (bq, bq), 0)
    col = lax.broadcasted_iota(jnp.int32, (bq, bq), 1)
    earlier = col < row

    casts = {}

    def kv_block(k_ref, v_ref, b, blk):
        key = (id(k_ref), b)
        if key not in casts:
            rows = slice(b * blk, (b + 1) * blk)
            casts[key] = (k_ref[rows, :].astype(BF16), v_ref[rows, :].astype(BF16))
        return casts[key]

    tails = []
    for i in range(nq):
        rows = slice(i * bq, (i + 1) * bq)
        q = (q_ref[rows, :] * (HEAD_DIM ** -0.5 * LOG2_E)).astype(BF16)
        new_left, past_left = i + 1, n_past
        carry = acc = None
        for _ in range(SB_STATIC_BLOCKS):
            if new_left:
                new_left -= 1
                kb, vb = kv_block(kn_ref, vn_ref, new_left, bq)
                carry, part = _sb_tile(q, kb, vb, tri_new, carry, earlier if new_left == i else None)
            elif past_left:
                past_left -= 1
                kb, vb = kv_block(kp_ref, vp_ref, past_left, bk)
                carry, part = _sb_tile(q, kb, vb, tri_past, carry, None)
            else:
                break
            acc = part if acc is None else acc + part
        o_ref[rows, :] = acc
        if new_left or past_left:
            tails.append((rows, q, carry, new_left, past_left))

    if tails:
        lowest = functools.reduce(jnp.minimum, [c for _, _, c, _, _ in tails])

        @pl.when(jnp.min(lowest) < SB_SKIP_CARRY)
        def _():
            for rows, q, carry, new_left, past_left in tails:
                @pl.when(jnp.min(carry) < SB_SKIP_CARRY)
                def _(rows=rows, q=q, carry=carry, new_left=new_left, past_left=past_left):
                    if new_left:
                        carry = _sb_tail(q, carry, rows, kn_ref, vn_ref, new_left, bq, tri_new, o_ref)
                    if past_left:
                        _sb_tail(q, carry, rows, kp_ref, vp_ref, past_left, bk, tri_past, o_ref)


def _sb_attention(q, k, v, k_past, v_past, layer, bq, bk):
    n, heads, t, dh = q.shape
    past = 0 if k_past is None else k_past.shape[3]
    assert t % bq == 0 and past % bk == 0
    q_spec = pl.BlockSpec((None, None, t, dh), lambda i, h: (i, h, 0, 0))
    kv_spec = pl.BlockSpec((None, None, None, t, dh), lambda i, h: (layer, i, h, 0, 0))
    in_specs = [q_spec, kv_spec, kv_spec]
    args = [q, k, v]
    if past:
        past_spec = pl.BlockSpec((None, None, None, past, dh), lambda i, h: (layer, i, h, 0, 0))
        in_specs += [past_spec, past_spec]
        args += [k_past, v_past]
    kernel = functools.partial(_sb_attn_kernel, bq=bq, bk=bk, past=past)
    return pl.pallas_call(
        kernel,
        grid=(n, heads),
        in_specs=in_specs,
        out_specs=pl.BlockSpec((None, t, dh), lambda i, h: (i, 0, h)),
        out_shape=jax.ShapeDtypeStruct((n, t, heads * dh), F32),
        compiler_params=_compiler_params(("parallel", "parallel"), 48),
        name="sb_attention",
    )(*args)


def _sb_out_kernel(a_ref, gate_ref, x_ref, wout_ref, g_ref, o_ref):
    a = (a_ref[...] * _silu(gate_ref[...])).astype(BF16)
    o = jnp.dot(a, wout_ref[...], preferred_element_type=F32)
    o_ref[...] = x_ref[...] + _rms(o, g_ref[...])


def _sb_out_proj(a, gate, x, wout, g, tc):
    n, t, d = x.shape
    w = gate.shape[2]
    return pl.pallas_call(
        _sb_out_kernel,
        grid=(n, t // tc),
        in_specs=[pl.BlockSpec((None, tc, w), lambda i, c: (i, c, 0)),
                  pl.BlockSpec((None, tc, w), lambda i, c: (i, c, 0)),
                  pl.BlockSpec((None, tc, d), lambda i, c: (i, c, 0)),
                  _resident((w, d)), _resident((1, d))],
        out_specs=pl.BlockSpec((None, tc, d), lambda i, c: (i, c, 0)),
        out_shape=jax.ShapeDtypeStruct((n, t, d), F32),
        compiler_params=_compiler_params(("parallel", "parallel"), 48),
        name="sb_out_proj",
    )(a, gate, x, wout, g)


def _sb_layer(x, k_past, v_past, kv_stacks, layer, n_layers, g_pre, g_post, w_in, w_out,
              row_tile, bq, bk):
    w = w_out.shape[0]
    wq, wk, wv, wg = (w_in[:, i * w:(i + 1) * w].astype(BF16) for i in range(4))
    q, k, v, gate = _sb_in_proj(x, g_pre[None, :], wq, wk, wv, wg, row_tile, layer, n_layers,
                                kv_stacks)
    a = _sb_attention(q, k, v, k_past, v_past, layer, bq, bk)
    x = _sb_out_proj(a, gate, x, w_out.astype(BF16), g_post[None, :], row_tile)
    return x, (k, v)


def _run_trunk(x, h0_re, h0_im, k_past, v_past, p, row_tile, scan_tile, bq, bk):
    depth = p["norm_pre"].shape[0]
    n_sb = p["w_in_sb"].shape[0]
    kv_stacks = None
    new_re, new_im = [], []
    for i in range(depth):
        j = i // 2
        if i % 2 == 0:
            x, hr, hi = _ssm_layer(
                x, h0_re[j], h0_im[j], p["norm_pre"][i], p["norm_post"][i], p["w_in_ssm"][j],
                p["ssm_a_re"][j], p["ssm_a_im"][j], p["ssm_log_step"][j], p["ssm_b_re"][j],
                p["ssm_b_im"][j], p["ssm_c_re"][j], p["ssm_c_im"][j], p["ssm_d"][j],
                p["w_glu"][j], p["w_out_ssm"][j], row_tile, scan_tile)
            new_re.append(hr)
            new_im.append(hi)
        else:
            x, kv_stacks = _sb_layer(x, k_past, v_past, kv_stacks, j, n_sb, p["norm_pre"][i],
                                     p["norm_post"][i], p["w_in_sb"][j], p["w_out_sb"][j],
                                     row_tile, bq, bk)
    return x, kv_stacks[0], kv_stacks[1], jnp.stack(new_re), jnp.stack(new_im)


def kernel(x_prompt, x_sample, cache_sb_k, cache_sb_v, state_ssm_re, state_ssm_im, norm_pre, norm_post, w_in_ssm, ssm_a_re, ssm_a_im, ssm_log_step, ssm_b_re, ssm_b_im, ssm_c_re, ssm_c_im, ssm_d, w_glu, w_out_ssm, w_in_sb, w_out_sb):
    p = dict(norm_pre=norm_pre, norm_post=norm_post, w_in_ssm=w_in_ssm, ssm_a_re=ssm_a_re,
             ssm_a_im=ssm_a_im, ssm_log_step=ssm_log_step, ssm_b_re=ssm_b_re, ssm_b_im=ssm_b_im,
             ssm_c_re=ssm_c_re, ssm_c_im=ssm_c_im, ssm_d=ssm_d, w_glu=w_glu,
             w_out_ssm=w_out_ssm, w_in_sb=w_in_sb, w_out_sb=w_out_sb)
    nb = x_prompt.shape[0]
    n_ssm = state_ssm_re.shape[0]
    zeros = jnp.zeros((n_ssm, nb) + state_ssm_re.shape[2:], F32)
    y_p, k_p, v_p, re_p, im_p = _run_trunk(x_prompt, zeros, zeros, None, None, p,
                                           row_tile=256, scan_tile=64, bq=256, bk=256)
    t_s = x_sample.shape[1]
    y_s, k_s, v_s, re_s, im_s = _run_trunk(x_sample, state_ssm_re, state_ssm_im,
                                           cache_sb_k, cache_sb_v, p,
                                           row_tile=t_s, scan_tile=t_s, bq=t_s, bk=256)
    return (y_p, y_s, k_p, v_p, re_p, im_p, k_s, v_s, re_s, im_s)
```

```python
import functools
import math

import jax
import jax.numpy as jnp
from jax import lax
from jax.experimental import pallas as pl
from jax.experimental.pallas import tpu as pltpu

F32 = jnp.float32
BF16 = jnp.bfloat16

RMS_EPS = 1e-6
SSM_GROUP = 16
SSM_STATE = 64
HEAD_DIM = 128

V7X_LANES = 128
V7X_VMEM_BYTES = 64 * 1024 * 1024

LOG2_E = math.log2(math.e)
SB_SKIP_CARRY = 104.0 * LOG2_E
SB_STATIC_BLOCKS = 2

SSM_CH_TILE = 2 * V7X_LANES
SSM_G_TILE = SSM_CH_TILE // SSM_GROUP
SSM_S_TILE = SSM_G_TILE * SSM_STATE
SCAN_COLS = V7X_LANES


def _compiler_params(semantics, vmem_mib):
    assert vmem_mib * 1024 * 1024 < V7X_VMEM_BYTES
    return pltpu.CompilerParams(dimension_semantics=semantics,
                                vmem_limit_bytes=vmem_mib * 1024 * 1024)


def _resident(shape):
    zeros = (0,) * len(shape)
    return pl.BlockSpec(shape, lambda *_: zeros, pipeline_mode=pl.Buffered(1))


def _rms(x, g):
    return x * lax.rsqrt(jnp.mean(x * x, axis=-1, keepdims=True) + RMS_EPS) * g


def _silu(x):
    return x * jax.nn.sigmoid(x)


def _ssm_in_kernel(x_ref, g_ref, wu_ref, wg_ref, u_ref, gate_ref):
    xn = _rms(x_ref[...], g_ref[...]).astype(BF16)
    u_ref[...] = jnp.dot(xn, wu_ref[...], preferred_element_type=F32)
    gate_ref[...] = jnp.dot(xn, wg_ref[...], preferred_element_type=F32)


def _ssm_in_proj(x, g, wu, wg, tc):
    n, t, d = x.shape
    w = wu.shape[1]
    return pl.pallas_call(
        _ssm_in_kernel,
        grid=(n, t // tc),
        in_specs=[pl.BlockSpec((None, tc, d), lambda i, c: (i, c, 0)),
                  _resident((1, d)), _resident((d, w)), _resident((d, w))],
        out_specs=[pl.BlockSpec((tc, w), lambda i, c: (c, i)),
                   pl.BlockSpec((None, tc, w), lambda i, c: (i, c, 0))],
        out_shape=[jax.ShapeDtypeStruct((t, n * w), F32),
                   jax.ShapeDtypeStruct((n, t, w), F32)],
        compiler_params=_compiler_params(("parallel", "parallel"), 48),
        name="ssm_in_proj",
    )(x, g, wu, wg)


def _ssm_mixer_kernel(u_ref, h0r_ref, h0i_ref, bm_ref, cm_ref, lbr_ref, lbi_ref, d_ref,
                      y_ref, htr_ref, hti_ref, sr_ref, si_ref, bu_ref, hs_ref, *, tc, n):
    c = pl.program_id(1)

    @pl.when(c == 0)
    def _():
        sr_ref[...] = h0r_ref[...]
        si_ref[...] = h0i_ref[...]

    u2 = u_ref[...].reshape(tc * n, SSM_CH_TILE)
    u2b = u2.astype(BF16)
    y = d_ref[...] * u2

    for q in range(SSM_S_TILE // SCAN_COLS):
        st_cols = slice(q * SCAN_COLS, (q + 1) * SCAN_COLS)
        re_cols = slice(2 * q * SCAN_COLS, (2 * q + 1) * SCAN_COLS)
        im_cols = slice((2 * q + 1) * SCAN_COLS, (2 * q + 2) * SCAN_COLS)
        both = slice(2 * q * SCAN_COLS, (2 * q + 2) * SCAN_COLS)
        bu_ref[:, both] = jnp.dot(u2b, bm_ref[:, both], preferred_element_type=F32)
        lr = jnp.broadcast_to(lbr_ref[:, st_cols], (n, SCAN_COLS))
        li = jnp.broadcast_to(lbi_ref[:, st_cols], (n, SCAN_COLS))
        hr = sr_ref[:, st_cols]
        hi = si_ref[:, st_cols]
        for t in range(tc):
            rows = slice(t * n, (t + 1) * n)
            hr, hi = (lr * hr - li * hi + bu_ref[rows, re_cols],
                      lr * hi + li * hr + bu_ref[rows, im_cols])
            hs_ref[rows, re_cols] = hr.astype(hs_ref.dtype)
            hs_ref[rows, im_cols] = hi.astype(hs_ref.dtype)
        sr_ref[:, st_cols] = hr
        si_ref[:, st_cols] = hi
        y = y + jnp.dot(hs_ref[:, both].astype(BF16), cm_ref[both, :],
                        preferred_element_type=F32)

    y_ref[...] = y.reshape(tc, n, SSM_CH_TILE)

    @pl.when(c == pl.num_programs(1) - 1)
    def _():
        htr_ref[...] = sr_ref[...]
        hti_ref[...] = si_ref[...]


def _ssm_mixer(u_tm, h0r, h0i, bm, cm, lbr, lbi, dskip, tc):
    t, n, w = u_tm.shape
    tiles = w // SSM_CH_TILE
    hs_dtype = BF16 if n % 16 == 0 else F32
    kernel = functools.partial(_ssm_mixer_kernel, tc=tc, n=n)
    return pl.pallas_call(
        kernel,
        grid=(tiles, t // tc),
        in_specs=[pl.BlockSpec((tc, n, SSM_CH_TILE), lambda j, c: (c, 0, j)),
                  pl.BlockSpec((n, SSM_S_TILE), lambda j, c: (0, j)),
                  pl.BlockSpec((n, SSM_S_TILE), lambda j, c: (0, j)),
                  pl.BlockSpec((None, SSM_CH_TILE, 2 * SSM_S_TILE), lambda j, c: (j, 0, 0)),
                  pl.BlockSpec((None, 2 * SSM_S_TILE, SSM_CH_TILE), lambda j, c: (j, 0, 0)),
                  pl.BlockSpec((None, 1, SSM_S_TILE), lambda j, c: (j, 0, 0)),
                  pl.BlockSpec((None, 1, SSM_S_TILE), lambda j, c: (j, 0, 0)),
                  pl.BlockSpec((1, SSM_CH_TILE), lambda j, c: (0, j))],
        out_specs=[pl.BlockSpec((tc, n, SSM_CH_TILE), lambda j, c: (c, 0, j)),
                   pl.BlockSpec((n, SSM_S_TILE), lambda j, c: (0, j)),
                   pl.BlockSpec((n, SSM_S_TILE), lambda j, c: (0, j))],
        out_shape=[jax.ShapeDtypeStruct((t, n, w), F32),
                   jax.ShapeDtypeStruct(h0r.shape, F32),
                   jax.ShapeDtypeStruct(h0i.shape, F32)],
        scratch_shapes=[pltpu.VMEM((n, SSM_S_TILE), F32),
                        pltpu.VMEM((n, SSM_S_TILE), F32),
                        pltpu.VMEM((tc * n, 2 * SSM_S_TILE), F32),
                        pltpu.VMEM((tc * n, 2 * SSM_S_TILE), hs_dtype)],
        compiler_params=_compiler_params(("parallel", "arbitrary"), 48),
        name="ssm_mixer",
    )(u_tm, h0r, h0i, bm, cm, lbr, lbi, dskip)


def _ssm_out_kernel(y_ref, gate_ref, x_ref, wglu_ref, wout_ref, g_ref, o_ref):
    y = jax.nn.gelu(y_ref[...])
    z = jnp.dot(y.astype(BF16), wglu_ref[...], preferred_element_type=F32)
    y = y * jax.nn.sigmoid(z)
    a = (y * _silu(gate_ref[...])).astype(BF16)
    o = jnp.dot(a, wout_ref[...], preferred_element_type=F32)
    o_ref[...] = x_ref[...] + _rms(o, g_ref[...])


def _ssm_out_proj(y_tm, gate, x, wglu, wout, g, tc):
    n, t, d = x.shape
    w = gate.shape[2]
    return pl.pallas_call(
        _ssm_out_kernel,
        grid=(n, t // tc),
        in_specs=[pl.BlockSpec((tc, w), lambda i, c: (c, i)),
                  pl.BlockSpec((None, tc, w), lambda i, c: (i, c, 0)),
                  pl.BlockSpec((None, tc, d), lambda i, c: (i, c, 0)),
                  _resident((w, w)), _resident((w, d)), _resident((1, d))],
        out_specs=pl.BlockSpec((None, tc, d), lambda i, c: (i, c, 0)),
        out_shape=jax.ShapeDtypeStruct((n, t, d), F32),
        compiler_params=_compiler_params(("parallel", "parallel"), 48),
        name="ssm_out_proj",
    )(y_tm, gate, x, wglu, wout, g)


def _ssm_weights(a_re, a_im, log_step, b_re, b_im, c_re, c_im):
    ar = a_re.astype(F32)
    ai = a_im.astype(F32)
    dt = jnp.exp(log_step.astype(F32))[:, None]
    mag = jnp.exp(ar * dt)
    ph = ai * dt
    lb_re = mag * jnp.cos(ph)
    lb_im = mag * jnp.sin(ph)
    nr = lb_re - 1.0
    ni = lb_im
    den = ar * ar + ai * ai
    fac_re = (nr * ar + ni * ai) / den
    fac_im = (ni * ar - nr * ai) / den
    br = b_re.astype(F32)
    bi = b_im.astype(F32)
    bbar_re = fac_re[..., None] * br - fac_im[..., None] * bi
    bbar_im = fac_re[..., None] * bi + fac_im[..., None] * br

    groups = a_re.shape[0]
    tiles = groups // SSM_G_TILE
    eye = jnp.eye(SSM_G_TILE, dtype=F32)

    def b_tile(b):
        b = b.reshape(tiles, SSM_G_TILE, SSM_STATE, SSM_GROUP).transpose(0, 1, 3, 2)
        blk = b[:, :, :, None, :] * eye[None, :, None, :, None]
        return blk.reshape(tiles, SSM_CH_TILE, SSM_S_TILE)

    def c_tile(cc):
        cc = cc.reshape(tiles, SSM_G_TILE, SSM_GROUP, SSM_STATE).transpose(0, 1, 3, 2)
        blk = cc[:, :, :, None, :] * eye[None, :, None, :, None]
        return blk.reshape(tiles, SSM_S_TILE, SSM_CH_TILE)

    passes = SSM_S_TILE // SCAN_COLS

    def interleave(re, im, axis):
        shape = list(re.shape)
        split = shape[:axis] + [passes, SCAN_COLS] + shape[axis + 1:]
        both = jnp.stack([re.reshape(split), im.reshape(split)], axis=axis + 1)
        shape[axis] *= 2
        return both.reshape(shape)

    bm = interleave(b_tile(bbar_re), b_tile(bbar_im), 2).astype(BF16)
    cm = interleave(c_tile(c_re.astype(F32)), -c_tile(c_im.astype(F32)), 1).astype(BF16)
    lbr = lb_re.reshape(tiles, 1, SSM_S_TILE)
    lbi = lb_im.reshape(tiles, 1, SSM_S_TILE)
    return bm, cm, lbr, lbi


def _ssm_layer(x, h0r, h0i, g_pre, g_post, w_in, a_re, a_im, log_step, b_re, b_im,
               c_re, c_im, d_skip, w_glu, w_out, row_tile, scan_tile):
    n, t, _ = x.shape
    w = w_glu.shape[0]
    wu = w_in[:, :w].astype(BF16)
    wg = w_in[:, w:].astype(BF16)
    bm, cm, lbr, lbi = _ssm_weights(a_re, a_im, log_step, b_re, b_im, c_re, c_im)
    u_tm, gate = _ssm_in_proj(x, g_pre[None, :], wu, wg, row_tile)
    y_tm, htr, hti = _ssm_mixer(u_tm.reshape(t, n, w), h0r.reshape(n, -1), h0i.reshape(n, -1),
                                bm, cm, lbr, lbi, d_skip[None, :], scan_tile)
    x = _ssm_out_proj(y_tm.reshape(t, n * w), gate, x, w_glu.astype(BF16), w_out.astype(BF16),
                      g_post[None, :], row_tile)
    return x, htr.reshape(h0r.shape), hti.reshape(h0i.shape)


def _sb_in_kernel(x_ref, g_ref, wq_ref, wk_ref, wv_ref, wg_ref, *rest):
    q_ref, k_ref, v_ref, gate_ref = rest[-4:]
    xn = _rms(x_ref[...], g_ref[...]).astype(BF16)
    for w_ref, o_ref, act in ((wq_ref, q_ref, None), (wk_ref, k_ref, None), (wv_ref, v_ref, None),
                              (wg_ref, gate_ref, _silu)):
        p = jnp.dot(xn, w_ref[...], preferred_element_type=F32)
        if act is not None:
            p = act(p)
        for h in range(o_ref.shape[0]):
            o_ref[h] = p[:, h * HEAD_DIM:(h + 1) * HEAD_DIM]


def _sb_in_proj(x, g, wq, wk, wv, wg, tc, layer, n_layers, kv_stacks):
    n, t, d = x.shape
    w = wq.shape[1]
    heads = w // HEAD_DIM
    q_spec = pl.BlockSpec((None, heads, tc, HEAD_DIM), lambda i, c: (i, 0, c, 0))
    kv_spec = pl.BlockSpec((None, None, heads, tc, HEAD_DIM), lambda i, c: (layer, i, 0, c, 0))
    kv_shape = jax.ShapeDtypeStruct((n_layers, n, heads, t, HEAD_DIM), F32)
    in_specs = [pl.BlockSpec((None, tc, d), lambda i, c: (i, c, 0)),
                _resident((1, d)), _resident((d, w)), _resident((d, w)),
                _resident((d, w)), _resident((d, w))]
    args = [x, g, wq, wk, wv, wg]
    aliases = {}
    if kv_stacks is not None:
        aliases = {len(args): 1, len(args) + 1: 2}
        in_specs += [pl.BlockSpec(memory_space=pl.ANY)] * 2
        args += list(kv_stacks)
    return pl.pallas_call(
        _sb_in_kernel,
        grid=(n, t // tc),
        in_specs=in_specs,
        out_specs=[q_spec, kv_spec, kv_spec, q_spec],
        out_shape=[jax.ShapeDtypeStruct((n, heads, t, HEAD_DIM), F32), kv_shape, kv_shape,
                   jax.ShapeDtypeStruct((n, heads, t, HEAD_DIM), F32)],
        input_output_aliases=aliases,
        compiler_params=_compiler_params(("parallel", "parallel"), 56),
        name="sb_in_proj",
    )(*args)


def _strict_lower_ones(size):
    j = lax.broadcasted_iota(jnp.int32, (2 * size, size), 0)
    s = lax.broadcasted_iota(jnp.int32, (2 * size, size), 1)
    return jnp.where(jnp.where(j >= size, j - size, j) > s, 1.0, 0.0).astype(BF16)


def _sb_tile(q, kb, vb, tri, carry, mask):
    z = lax.dot_general(q, kb, (((1,), (1,)), ((), ())), preferred_element_type=F32)
    neg_abs = lax.bitcast_convert_type(
        lax.bitcast_convert_type(z, jnp.uint32) | jnp.uint32(0x80000000), F32)
    sp = jnp.maximum(z, 0.0) + jnp.log(1.0 + jnp.exp2(neg_abs)) * LOG2_E
    if mask is not None:
        sp = jnp.where(mask, sp, 0.0)
    hi = sp.astype(BF16)
    lo = (sp - hi.astype(F32)).astype(BF16)
    rest = jnp.dot(jnp.concatenate([hi, lo], axis=1), tri, preferred_element_type=F32)
    arg = z - sp - rest
    total = jnp.sum(sp, axis=-1, keepdims=True)
    if carry is not None:
        arg = arg - carry
        total = total + carry
    w = jnp.exp2(arg)
    if mask is not None:
        w = jnp.where(mask, w, 0.0)
    return total, jnp.dot(w.astype(BF16), vb, preferred_element_type=F32)


def _sb_tail(q, carry, rows, k_ref, v_ref, n_blocks, blk, tri, acc_ref):
    def body(j, carry):
        off = pl.multiple_of((n_blocks - 1 - j) * blk, blk)
        kb = k_ref[pl.ds(off, blk), :].astype(BF16)
        vb = v_ref[pl.ds(off, blk), :].astype(BF16)
        carry, part = _sb_tile(q, kb, vb, tri, carry, None)
        acc_ref[rows, :] += part
        return carry

    return lax.fori_loop(0, n_blocks, body, carry)


def _sb_attn_kernel(*refs, bq, bk, past):
    if past:
        q_ref, kn_ref, vn_ref, kp_ref, vp_ref, gate_ref, o_ref, acc_ref = refs
    else:
        q_ref, kn_ref, vn_ref, gate_ref, o_ref, acc_ref = refs
    t = q_ref.shape[0]
    nq = t // bq
    n_past = past // bk
    tri_new = _strict_lower_ones(bq)
    tri_past = tri_new if bk == bq else _strict_lower_ones(bk)
    row = lax.broadcasted_iota(jnp.int32, (bq, bq), 0)
    col = lax.broadcasted_iota(jnp.int32, (bq, bq), 1)
    earlier = col < row

    casts = {}

    def kv_block(k_ref, v_ref, b, blk):
        key = (id(k_ref), b)
        if key not in casts:
            rows = slice(b * blk, (b + 1) * blk)
            casts[key] = (k_ref[rows, :].astype(BF16), v_ref[rows, :].astype(BF16))
        return casts[key]

    tails = []
    for i in range(nq):
        rows = slice(i * bq, (i + 1) * bq)
        q = (q_ref[rows, :] * (HEAD_DIM ** -0.5 * LOG2_E)).astype(BF16)
        new_left, past_left = i + 1, n_past
        carry = acc = None
        for _ in range(SB_STATIC_BLOCKS):
            if new_left:
                new_left -= 1
                kb, vb = kv_block(kn_ref, vn_ref, new_left, bq)
                carry, part = _sb_tile(q, kb, vb, tri_new, carry, earlier if new_left == i else None)
            elif past_left:
                past_left -= 1
                kb, vb = kv_block(kp_ref, vp_ref, past_left, bk)
                carry, part = _sb_tile(q, kb, vb, tri_past, carry, None)
            else:
                break
            acc = part if acc is None else acc + part
        acc_ref[rows, :] = acc
        if new_left or past_left:
            tails.append((rows, q, carry, new_left, past_left))

    if tails:
        lowest = functools.reduce(jnp.minimum, [c for _, _, c, _, _ in tails])

        @pl.when(jnp.min(lowest) < SB_SKIP_CARRY)
        def _():
            for rows, q, carry, new_left, past_left in tails:
                @pl.when(jnp.min(carry) < SB_SKIP_CARRY)
                def _(rows=rows, q=q, carry=carry, new_left=new_left, past_left=past_left):
                    if new_left:
                        carry = _sb_tail(q, carry, rows, kn_ref, vn_ref, new_left, bq, tri_new, acc_ref)
                    if past_left:
                        _sb_tail(q, carry, rows, kp_ref, vp_ref, past_left, bk, tri_past, acc_ref)

    o_ref[...] = (acc_ref[...] * gate_ref[...]).astype(BF16)


def _sb_attention(q, k, v, k_past, v_past, gate_act, layer, bq, bk):
    n, heads, t, dh = q.shape
    past = 0 if k_past is None else k_past.shape[3]
    assert t % bq == 0 and past % bk == 0
    q_spec = pl.BlockSpec((None, None, t, dh), lambda i, h: (i, h, 0, 0))
    kv_spec = pl.BlockSpec((None, None, None, t, dh), lambda i, h: (layer, i, h, 0, 0))
    in_specs = [q_spec, kv_spec, kv_spec]
    args = [q, k, v]
    if past:
        past_spec = pl.BlockSpec((None, None, None, past, dh), lambda i, h: (layer, i, h, 0, 0))
        in_specs += [past_spec, past_spec]
        args += [k_past, v_past]
    in_specs.append(q_spec)
    args.append(gate_act)
    kernel = functools.partial(_sb_attn_kernel, bq=bq, bk=bk, past=past)
    return pl.pallas_call(
        kernel,
        grid=(n, heads),
        in_specs=in_specs,
        out_specs=pl.BlockSpec((None, t, dh), lambda i, h: (i, 0, h)),
        out_shape=jax.ShapeDtypeStruct((n, t, heads * dh), BF16),
        scratch_shapes=[pltpu.VMEM((t, dh), F32)],
        compiler_params=_compiler_params(("parallel", "parallel"), 48),
        name="sb_attention",
    )(*args)


def _sb_out_kernel(a_ref, x_ref, wout_ref, g_ref, o_ref):
    o = jnp.dot(a_ref[...], wout_ref[...], preferred_element_type=F32)
    o_ref[...] = x_ref[...] + _rms(o, g_ref[...])


def _sb_out_proj(a, x, wout, g, tc):
    n, t, d = x.shape
    w = a.shape[2]
    return pl.pallas_call(
        _sb_out_kernel,
        grid=(n, t // tc),
        in_specs=[pl.BlockSpec((None, tc, w), lambda i, c: (i, c, 0)),
                  pl.BlockSpec((None, tc, d), lambda i, c: (i, c, 0)),
                  _resident((w, d)), _resident((1, d))],
        out_specs=pl.BlockSpec((None, tc, d), lambda i, c: (i, c, 0)),
        out_shape=jax.ShapeDtypeStruct((n, t, d), F32),
        compiler_params=_compiler_params(("parallel", "parallel"), 48),
        name="sb_out_proj",
    )(a, x, wout, g)


def _sb_layer(x, k_past, v_past, kv_stacks, layer, n_layers, g_pre, g_post, w_in, w_out,
              row_tile, bq, bk):
    w = w_out.shape[0]
    wq, wk, wv, wg = (w_in[:, i * w:(i + 1) * w].astype(BF16) for i in range(4))
    q, k, v, gate_act = _sb_in_proj(x, g_pre[None, :], wq, wk, wv, wg, row_tile, layer, n_layers,
                                    kv_stacks)
    a = _sb_attention(q, k, v, k_past, v_past, gate_act, layer, bq, bk)
    x = _sb_out_proj(a, x, w_out.astype(BF16), g_post[None, :], row_tile)
    return x, (k, v)


def _run_trunk(x, h0_re, h0_im, k_past, v_past, p, row_tile, scan_tile, bq, bk):
    depth = p["norm_pre"].shape[0]
    n_sb = p["w_in_sb"].shape[0]
    kv_stacks = None
    new_re, new_im = [], []
    for i in range(depth):
        j = i // 2
        if i % 2 == 0:
            x, hr, hi = _ssm_layer(
                x, h0_re[j], h0_im[j], p["norm_pre"][i], p["norm_post"][i], p["w_in_ssm"][j],
                p["ssm_a_re"][j], p["ssm_a_im"][j], p["ssm_log_step"][j], p["ssm_b_re"][j],
                p["ssm_b_im"][j], p["ssm_c_re"][j], p["ssm_c_im"][j], p["ssm_d"][j],
                p["w_glu"][j], p["w_out_ssm"][j], row_tile, scan_tile)
            new_re.append(hr)
            new_im.append(hi)
        else:
            x, kv_stacks = _sb_layer(x, k_past, v_past, kv_stacks, j, n_sb, p["norm_pre"][i],
                                     p["norm_post"][i], p["w_in_sb"][j], p["w_out_sb"][j],
                                     row_tile, bq, bk)
    return x, kv_stacks[0], kv_stacks[1], jnp.stack(new_re), jnp.stack(new_im)


def kernel(x_prompt, x_sample, cache_sb_k, cache_sb_v, state_ssm_re, state_ssm_im, norm_pre, norm_post, w_in_ssm, ssm_a_re, ssm_a_im, ssm_log_step, ssm_b_re, ssm_b_im, ssm_c_re, ssm_c_im, ssm_d, w_glu, w_out_ssm, w_in_sb, w_out_sb):
    p = dict(norm_pre=norm_pre, norm_post=norm_post, w_in_ssm=w_in_ssm, ssm_a_re=ssm_a_re,
             ssm_a_im=ssm_a_im, ssm_log_step=ssm_log_step, ssm_b_re=ssm_b_re, ssm_b_im=ssm_b_im,
             ssm_c_re=ssm_c_re, ssm_c_im=ssm_c_im, ssm_d=ssm_d, w_glu=w_glu,
             w_out_ssm=w_out_ssm, w_in_sb=w_in_sb, w_out_sb=w_out_sb)
    nb = x_prompt.shape[0]
    n_ssm = state_ssm_re.shape[0]
    zeros = jnp.zeros((n_ssm, nb) + state_ssm_re.shape[2:], F32)
    y_p, k_p, v_p, re_p, im_p = _run_trunk(x_prompt, zeros, zeros, None, None, p,
                                           row_tile=256, scan_tile=64, bq=256, bk=256)
    t_s = x_sample.shape[1]
    y_s, k_s, v_s, re_s, im_s = _run_trunk(x_sample, state_ssm_re, state_ssm_im,
                                           cache_sb_k, cache_sb_v, p,
                                           row_tile=t_s, scan_tile=t_s, bq=t_s, bk=256)
    return (y_p, y_s, k_p, v_p, re_p, im_p, k_s, v_s, re_s, im_s)
```

```python
import functools
import math
import typing

import jax
import jax.numpy as jnp
from jax import lax
from jax.experimental import pallas as pl
from jax.experimental.pallas import tpu as pltpu

F32 = jnp.float32
BF16 = jnp.bfloat16

RMS_EPS = 1e-6
SSM_GROUP = 16
SSM_STATE = 64
HEAD_DIM = 128

V7X_LANES = 128
V7X_VMEM_BYTES = 64 * 1024 * 1024

LOG2_E = math.log2(math.e)
SB_SKIP_CARRY = 104.0 * LOG2_E
SB_STATIC_BLOCKS = 2

SSM_CH_TILE = V7X_LANES
SSM_G_TILE = SSM_CH_TILE // SSM_GROUP
SSM_S_TILE = SSM_G_TILE * SSM_STATE
SCAN_COLS = V7X_LANES


def _compiler_params(semantics, vmem_mib):
    assert vmem_mib * 1024 * 1024 < V7X_VMEM_BYTES
    return pltpu.CompilerParams(dimension_semantics=semantics,
                                vmem_limit_bytes=vmem_mib * 1024 * 1024)


def _resident(shape):
    zeros = (0,) * len(shape)
    return pl.BlockSpec(shape, lambda *_: zeros, pipeline_mode=pl.Buffered(1))


def _rms(x, g):
    return x * lax.rsqrt(jnp.mean(x * x, axis=-1, keepdims=True) + RMS_EPS) * g


def _silu(x):
    return x * jax.nn.sigmoid(x)


def _ssm_in_kernel(x_ref, g_ref, wu_ref, wg_ref, u_ref, gate_ref):
    xn = _rms(x_ref[...], g_ref[...]).astype(BF16)
    u_ref[...] = jnp.dot(xn, wu_ref[...], preferred_element_type=F32)
    gate_ref[...] = jnp.dot(xn, wg_ref[...], preferred_element_type=F32)


def _ssm_in_proj(x, g, wu, wg, tc):
    n, t, d = x.shape
    w = wu.shape[1]
    return pl.pallas_call(
        _ssm_in_kernel,
        grid=(n, t // tc),
        in_specs=[pl.BlockSpec((None, tc, d), lambda i, c: (i, c, 0)),
                  _resident((1, d)), _resident((d, w)), _resident((d, w))],
        out_specs=[pl.BlockSpec((tc, w), lambda i, c: (c, i)),
                   pl.BlockSpec((None, tc, w), lambda i, c: (i, c, 0))],
        out_shape=[jax.ShapeDtypeStruct((t, n * w), F32),
                   jax.ShapeDtypeStruct((n, t, w), F32)],
        compiler_params=_compiler_params(("parallel", "parallel"), 48),
        name="ssm_in_proj",
    )(x, g, wu, wg)


def _ssm_mixer_kernel(u_ref, h0r_ref, h0i_ref, bm_ref, cm_ref, lbr_ref, lbi_ref, d_ref,
                      y_ref, htr_ref, hti_ref, sr_ref, si_ref, bu_ref, hs_ref, *, tc, n):
    c = pl.program_id(1)

    @pl.when(c == 0)
    def _():
        sr_ref[...] = h0r_ref[...]
        si_ref[...] = h0i_ref[...]

    u2 = u_ref[...].reshape(tc * n, SSM_CH_TILE)
    u2b = u2.astype(BF16)
    y = d_ref[...] * u2

    for q in range(SSM_S_TILE // SCAN_COLS):
        st_cols = slice(q * SCAN_COLS, (q + 1) * SCAN_COLS)
        both = slice(2 * q * SCAN_COLS, (2 * q + 2) * SCAN_COLS)
        slot = q % bu_ref.shape[0]
        bu_ref[slot] = jnp.dot(u2b, bm_ref[:, both], preferred_element_type=F32)
        lr = jnp.broadcast_to(lbr_ref[:, st_cols], (n, SCAN_COLS))
        li = jnp.broadcast_to(lbi_ref[:, st_cols], (n, SCAN_COLS))
        hr = sr_ref[:, st_cols]
        hi = si_ref[:, st_cols]
        for t in range(tc):
            rows = slice(t * n, (t + 1) * n)
            hr, hi = (lr * hr - li * hi + bu_ref[slot, rows, :SCAN_COLS],
                      lr * hi + li * hr + bu_ref[slot, rows, SCAN_COLS:])
            hs_ref[slot, rows, :SCAN_COLS] = hr.astype(hs_ref.dtype)
            hs_ref[slot, rows, SCAN_COLS:] = hi.astype(hs_ref.dtype)
        sr_ref[:, st_cols] = hr
        si_ref[:, st_cols] = hi
        y = y + jnp.dot(hs_ref[slot].astype(BF16), cm_ref[both, :],
                        preferred_element_type=F32)

    y_ref[...] = y.reshape(tc, n, SSM_CH_TILE)

    @pl.when(c == pl.num_programs(1) - 1)
    def _():
        htr_ref[...] = sr_ref[...]
        hti_ref[...] = si_ref[...]


def _ssm_mixer(u_tm, h0r, h0i, bm, cm, lbr, lbi, dskip, tc):
    t, n, w = u_tm.shape
    tiles = w // SSM_CH_TILE
    hs_dtype = BF16 if n % 16 == 0 else F32
    kernel = functools.partial(_ssm_mixer_kernel, tc=tc, n=n)
    return pl.pallas_call(
        kernel,
        grid=(tiles, t // tc),
        in_specs=[pl.BlockSpec((tc, n, SSM_CH_TILE), lambda j, c: (c, 0, j)),
                  pl.BlockSpec((n, SSM_S_TILE), lambda j, c: (0, j)),
                  pl.BlockSpec((n, SSM_S_TILE), lambda j, c: (0, j)),
                  pl.BlockSpec((None, SSM_CH_TILE, 2 * SSM_S_TILE), lambda j, c: (j, 0, 0)),
                  pl.BlockSpec((None, 2 * SSM_S_TILE, SSM_CH_TILE), lambda j, c: (j, 0, 0)),
                  pl.BlockSpec((None, 1, SSM_S_TILE), lambda j, c: (j, 0, 0)),
                  pl.BlockSpec((None, 1, SSM_S_TILE), lambda j, c: (j, 0, 0)),
                  pl.BlockSpec((1, SSM_CH_TILE), lambda j, c: (0, j))],
        out_specs=[pl.BlockSpec((tc, n, SSM_CH_TILE), lambda j, c: (c, 0, j)),
                   pl.BlockSpec((n, SSM_S_TILE), lambda j, c: (0, j)),
                   pl.BlockSpec((n, SSM_S_TILE), lambda j, c: (0, j))],
        out_shape=[jax.ShapeDtypeStruct((t, n, w), F32),
                   jax.ShapeDtypeStruct(h0r.shape, F32),
                   jax.ShapeDtypeStruct(h0i.shape, F32)],
        scratch_shapes=[pltpu.VMEM((n, SSM_S_TILE), F32),
                        pltpu.VMEM((n, SSM_S_TILE), F32),
                        pltpu.VMEM((2, tc * n, 2 * SCAN_COLS), F32),
                        pltpu.VMEM((2, tc * n, 2 * SCAN_COLS), hs_dtype)],
        compiler_params=_compiler_params(("parallel", "arbitrary"), 48),
        name="ssm_mixer",
    )(u_tm, h0r, h0i, bm, cm, lbr, lbi, dskip)


def _ssm_out_kernel(y_ref, gate_ref, x_ref, wglu_ref, wout_ref, g_ref, o_ref):
    y = jax.nn.gelu(y_ref[...])
    z = jnp.dot(y.astype(BF16), wglu_ref[...], preferred_element_type=F32)
    y = y * jax.nn.sigmoid(z)
    a = (y * _silu(gate_ref[...])).astype(BF16)
    o = jnp.dot(a, wout_ref[...], preferred_element_type=F32)
    o_ref[...] = x_ref[...] + _rms(o, g_ref[...])


def _ssm_out_proj(y_tm, gate, x, wglu, wout, g, tc):
    n, t, d = x.shape
    w = gate.shape[2]
    return pl.pallas_call(
        _ssm_out_kernel,
        grid=(n, t // tc),
        in_specs=[pl.BlockSpec((tc, w), lambda i, c: (c, i)),
                  pl.BlockSpec((None, tc, w), lambda i, c: (i, c, 0)),
                  pl.BlockSpec((None, tc, d), lambda i, c: (i, c, 0)),
                  _resident((w, w)), _resident((w, d)), _resident((1, d))],
        out_specs=pl.BlockSpec((None, tc, d), lambda i, c: (i, c, 0)),
        out_shape=jax.ShapeDtypeStruct((n, t, d), F32),
        compiler_params=_compiler_params(("parallel", "parallel"), 48),
        name="ssm_out_proj",
    )(y_tm, gate, x, wglu, wout, g)


def _ssm_weights(a_re, a_im, log_step, b_re, b_im, c_re, c_im):
    ar = a_re.astype(F32)
    ai = a_im.astype(F32)
    dt = jnp.exp(log_step.astype(F32))[:, None]
    mag = jnp.exp(ar * dt)
    ph = ai * dt
    lb_re = mag * jnp.cos(ph)
    lb_im = mag * jnp.sin(ph)
    nr = lb_re - 1.0
    ni = lb_im
    den = ar * ar + ai * ai
    fac_re = (nr * ar + ni * ai) / den
    fac_im = (ni * ar - nr * ai) / den
    br = b_re.astype(F32)
    bi = b_im.astype(F32)
    bbar_re = fac_re[..., None] * br - fac_im[..., None] * bi
    bbar_im = fac_re[..., None] * bi + fac_im[..., None] * br

    groups = a_re.shape[0]
    tiles = groups // SSM_G_TILE
    eye = jnp.eye(SSM_G_TILE, dtype=F32)

    def b_tile(b):
        b = b.reshape(tiles, SSM_G_TILE, SSM_STATE, SSM_GROUP).transpose(0, 1, 3, 2)
        blk = b[:, :, :, None, :] * eye[None, :, None, :, None]
        return blk.reshape(tiles, SSM_CH_TILE, SSM_S_TILE)

    def c_tile(cc):
        cc = cc.reshape(tiles, SSM_G_TILE, SSM_GROUP, SSM_STATE).transpose(0, 1, 3, 2)
        blk = cc[:, :, :, None, :] * eye[None, :, None, :, None]
        return blk.reshape(tiles, SSM_S_TILE, SSM_CH_TILE)

    passes = SSM_S_TILE // SCAN_COLS

    def interleave(re, im, axis):
        shape = list(re.shape)
        split = shape[:axis] + [passes, SCAN_COLS] + shape[axis + 1:]
        both = jnp.stack([re.reshape(split), im.reshape(split)], axis=axis + 1)
        shape[axis] *= 2
        return both.reshape(shape)

    bm = interleave(b_tile(bbar_re), b_tile(bbar_im), 2).astype(BF16)
    cm = interleave(c_tile(c_re.astype(F32)), -c_tile(c_im.astype(F32)), 1).astype(BF16)
    lbr = lb_re.reshape(tiles, 1, SSM_S_TILE)
    lbi = lb_im.reshape(tiles, 1, SSM_S_TILE)
    return bm, cm, lbr, lbi


def _ssm_layer(x, h0r, h0i, g_pre, g_post, w_in, a_re, a_im, log_step, b_re, b_im,
               c_re, c_im, d_skip, w_glu, w_out, tiles):
    n, t, _ = x.shape
    w = w_glu.shape[0]
    wu = w_in[:, :w].astype(BF16)
    wg = w_in[:, w:].astype(BF16)
    bm, cm, lbr, lbi = _ssm_weights(a_re, a_im, log_step, b_re, b_im, c_re, c_im)
    u_tm, gate = _ssm_in_proj(x, g_pre[None, :], wu, wg, tiles.rows)
    y_tm, htr, hti = _ssm_mixer(u_tm.reshape(t, n, w), h0r.reshape(n, -1), h0i.reshape(n, -1),
                                bm, cm, lbr, lbi, d_skip[None, :], tiles.scan)
    x = _ssm_out_proj(y_tm.reshape(t, n * w), gate, x, w_glu.astype(BF16), w_out.astype(BF16),
                      g_post[None, :], tiles.rows)
    return x, htr.reshape(h0r.shape), hti.reshape(h0i.shape)


def _sb_in_kernel(x_ref, g_ref, wq_ref, wk_ref, wv_ref, wg_ref, *rest):
    q_ref, k_ref, v_ref, gate_ref = rest[-4:]
    xn = _rms(x_ref[...], g_ref[...]).astype(BF16)
    for w_ref, o_ref, act in ((wq_ref, q_ref, None), (wk_ref, k_ref, None), (wv_ref, v_ref, None),
                              (wg_ref, gate_ref, _silu)):
        p = jnp.dot(xn, w_ref[...], preferred_element_type=F32)
        if act is not None:
            p = act(p)
        for h in range(o_ref.shape[0]):
            o_ref[h] = p[:, h * HEAD_DIM:(h + 1) * HEAD_DIM]


def _sb_in_proj(x, g, wq, wk, wv, wg, tc, layer, n_layers, kv_stacks):
    n, t, d = x.shape
    w = wq.shape[1]
    heads = w // HEAD_DIM
    q_spec = pl.BlockSpec((None, heads, tc, HEAD_DIM), lambda i, c: (i, 0, c, 0))
    kv_spec = pl.BlockSpec((None, None, heads, tc, HEAD_DIM), lambda i, c: (layer, i, 0, c, 0))
    kv_shape = jax.ShapeDtypeStruct((n_layers, n, heads, t, HEAD_DIM), F32)
    in_specs = [pl.BlockSpec((None, tc, d), lambda i, c: (i, c, 0)),
                _resident((1, d)), _resident((d, w)), _resident((d, w)),
                _resident((d, w)), _resident((d, w))]
    args = [x, g, wq, wk, wv, wg]
    aliases = {}
    if kv_stacks is not None:
        aliases = {len(args): 1, len(args) + 1: 2}
        in_specs += [pl.BlockSpec(memory_space=pl.ANY)] * 2
        args += list(kv_stacks)
    return pl.pallas_call(
        _sb_in_kernel,
        grid=(n, t // tc),
        in_specs=in_specs,
        out_specs=[q_spec, kv_spec, kv_spec, q_spec],
        out_shape=[jax.ShapeDtypeStruct((n, heads, t, HEAD_DIM), F32), kv_shape, kv_shape,
                   jax.ShapeDtypeStruct((n, heads, t, HEAD_DIM), F32)],
        input_output_aliases=aliases,
        compiler_params=_compiler_params(("parallel", "parallel"), 56),
        name="sb_in_proj",
    )(*args)


def _strict_lower_ones(size):
    j = lax.broadcasted_iota(jnp.int32, (2 * size, size), 0)
    s = lax.broadcasted_iota(jnp.int32, (2 * size, size), 1)
    return jnp.where(jnp.where(j >= size, j - size, j) > s, 1.0, 0.0).astype(BF16)


def _sb_tile(q, kb, vb, tri, carry, mask):
    z = lax.dot_general(q, kb, (((1,), (1,)), ((), ())), preferred_element_type=F32)
    neg_abs = lax.bitcast_convert_type(
        lax.bitcast_convert_type(z, jnp.uint32) | jnp.uint32(0x80000000), F32)
    sp = jnp.maximum(z, 0.0) + jnp.log(1.0 + jnp.exp2(neg_abs)) * LOG2_E
    if mask is not None:
        sp = jnp.where(mask, sp, 0.0)
    hi = sp.astype(BF16)
    lo = (sp - hi.astype(F32)).astype(BF16)
    rest = jnp.dot(jnp.concatenate([hi, lo], axis=1), tri, preferred_element_type=F32)
    arg = z - sp - rest
    total = jnp.sum(sp, axis=-1, keepdims=True)
    if carry is not None:
        arg = arg - carry
        total = total + carry
    w = jnp.exp2(arg)
    if mask is not None:
        w = jnp.where(mask, w, 0.0)
    return total, jnp.dot(w.astype(BF16), vb, preferred_element_type=F32)


def _sb_tail(q, carry, rows, k_ref, v_ref, n_blocks, blk, tri, acc_ref):
    def body(j, carry):
        off = pl.multiple_of((n_blocks - 1 - j) * blk, blk)
        kb = k_ref[pl.ds(off, blk), :].astype(BF16)
        vb = v_ref[pl.ds(off, blk), :].astype(BF16)
        carry, part = _sb_tile(q, kb, vb, tri, carry, None)
        acc_ref[rows, :] += part
        return carry

    return lax.fori_loop(0, n_blocks, body, carry)


def _sb_attn_kernel(*refs, bq, bk, past):
    if past:
        q_ref, kn_ref, vn_ref, kp_ref, vp_ref, gate_ref, o_ref, acc_ref = refs
    else:
        q_ref, kn_ref, vn_ref, gate_ref, o_ref, acc_ref = refs
    t = q_ref.shape[0]
    nq = t // bq
    n_past = past // bk
    tri_new = _strict_lower_ones(bq)
    tri_past = tri_new if bk == bq else _strict_lower_ones(bk)
    row = lax.broadcasted_iota(jnp.int32, (bq, bq), 0)
    col = lax.broadcasted_iota(jnp.int32, (bq, bq), 1)
    earlier = col < row

    casts = {}

    def kv_block(k_ref, v_ref, b, blk):
        key = (id(k_ref), b)
        if key not in casts:
            rows = slice(b * blk, (b + 1) * blk)
            casts[key] = (k_ref[rows, :].astype(BF16), v_ref[rows, :].astype(BF16))
        return casts[key]

    tails = []
    for i in range(nq):
        rows = slice(i * bq, (i + 1) * bq)
        q = (q_ref[rows, :] * (HEAD_DIM ** -0.5 * LOG2_E)).astype(BF16)
        new_left, past_left = i + 1, n_past
        carry = acc = None
        for _ in range(SB_STATIC_BLOCKS):
            if new_left:
                new_left -= 1
                kb, vb = kv_block(kn_ref, vn_ref, new_left, bq)
                carry, part = _sb_tile(q, kb, vb, tri_new, carry, earlier if new_left == i else None)
            elif past_left:
                past_left -= 1
                kb, vb = kv_block(kp_ref, vp_ref, past_left, bk)
                carry, part = _sb_tile(q, kb, vb, tri_past, carry, None)
            else:
                break
            acc = part if acc is None else acc + part
        acc_ref[rows, :] = acc
        if new_left or past_left:
            tails.append((rows, q, carry, new_left, past_left))

    if tails:
        lowest = functools.reduce(jnp.minimum, [c for _, _, c, _, _ in tails])

        @pl.when(jnp.min(lowest) < SB_SKIP_CARRY)
        def _():
            for rows, q, carry, new_left, past_left in tails:
                @pl.when(jnp.min(carry) < SB_SKIP_CARRY)
                def _(rows=rows, q=q, carry=carry, new_left=new_left, past_left=past_left):
                    if new_left:
                        carry = _sb_tail(q, carry, rows, kn_ref, vn_ref, new_left, bq, tri_new, acc_ref)
                    if past_left:
                        _sb_tail(q, carry, rows, kp_ref, vp_ref, past_left, bk, tri_past, acc_ref)

    o_ref[...] = (acc_ref[...] * gate_ref[...]).astype(BF16)


def _sb_attention(q, k, v, k_past, v_past, gate_act, layer, bq, bk):
    n, heads, t, dh = q.shape
    past = 0 if k_past is None else k_past.shape[3]
    assert t % bq == 0 and past % bk == 0
    q_spec = pl.BlockSpec((None, None, t, dh), lambda i, h: (i, h, 0, 0))
    kv_spec = pl.BlockSpec((None, None, None, t, dh), lambda i, h: (layer, i, h, 0, 0))
    in_specs = [q_spec, kv_spec, kv_spec]
    args = [q, k, v]
    if past:
        past_spec = pl.BlockSpec((None, None, None, past, dh), lambda i, h: (layer, i, h, 0, 0))
        in_specs += [past_spec, past_spec]
        args += [k_past, v_past]
    in_specs.append(q_spec)
    args.append(gate_act)
    kernel = functools.partial(_sb_attn_kernel, bq=bq, bk=bk, past=past)
    return pl.pallas_call(
        kernel,
        grid=(n, heads),
        in_specs=in_specs,
        out_specs=pl.BlockSpec((None, t, dh), lambda i, h: (i, 0, h)),
        out_shape=jax.ShapeDtypeStruct((n, t, heads * dh), BF16),
        scratch_shapes=[pltpu.VMEM((t, dh), F32)],
        compiler_params=_compiler_params(("parallel", "parallel"), 48),
        name="sb_attention",
    )(*args)


def _sb_out_kernel(a_ref, x_ref, wout_ref, g_ref, o_ref):
    o = jnp.dot(a_ref[...], wout_ref[...], preferred_element_type=F32)
    o_ref[...] = x_ref[...] + _rms(o, g_ref[...])


def _sb_out_proj(a, x, wout, g, tc):
    n, t, d = x.shape
    w = a.shape[2]
    return pl.pallas_call(
        _sb_out_kernel,
        grid=(n, t // tc),
        in_specs=[pl.BlockSpec((None, tc, w), lambda i, c: (i, c, 0)),
                  pl.BlockSpec((None, tc, d), lambda i, c: (i, c, 0)),
                  _resident((w, d)), _resident((1, d))],
        out_specs=pl.BlockSpec((None, tc, d), lambda i, c: (i, c, 0)),
        out_shape=jax.ShapeDtypeStruct((n, t, d), F32),
        compiler_params=_compiler_params(("parallel", "parallel"), 48),
        name="sb_out_proj",
    )(a, x, wout, g)


def _sb_layer(x, k_past, v_past, kv_stacks, layer, n_layers, g_pre, g_post, w_in, w_out,
              tiles):
    w = w_out.shape[0]
    wq, wk, wv, wg = (w_in[:, i * w:(i + 1) * w].astype(BF16) for i in range(4))
    q, k, v, gate_act = _sb_in_proj(x, g_pre[None, :], wq, wk, wv, wg, tiles.sb_in_rows, layer,
                                    n_layers, kv_stacks)
    a = _sb_attention(q, k, v, k_past, v_past, gate_act, layer, tiles.queries, tiles.keys)
    x = _sb_out_proj(a, x, w_out.astype(BF16), g_post[None, :], tiles.rows)
    return x, (k, v)


class _TrunkTiles(typing.NamedTuple):
    rows: int
    sb_in_rows: int
    scan: int
    queries: int
    keys: int


def _trunk_tiles(t):
    return _TrunkTiles(rows=min(512, t), sb_in_rows=min(256, t), scan=min(256, t),
                       queries=min(256, t), keys=256)


def _run_trunk(x, h0_re, h0_im, k_past, v_past, p):
    depth = p["norm_pre"].shape[0]
    n_sb = p["w_in_sb"].shape[0]
    tiles = _trunk_tiles(x.shape[1])
    kv_stacks = None
    new_re, new_im = [], []
    for i in range(depth):
        j = i // 2
        if i % 2 == 0:
            x, hr, hi = _ssm_layer(
                x, h0_re[j], h0_im[j], p["norm_pre"][i], p["norm_post"][i], p["w_in_ssm"][j],
                p["ssm_a_re"][j], p["ssm_a_im"][j], p["ssm_log_step"][j], p["ssm_b_re"][j],
                p["ssm_b_im"][j], p["ssm_c_re"][j], p["ssm_c_im"][j], p["ssm_d"][j],
                p["w_glu"][j], p["w_out_ssm"][j], tiles)
            new_re.append(hr)
            new_im.append(hi)
        else:
            x, kv_stacks = _sb_layer(x, k_past, v_past, kv_stacks, j, n_sb, p["norm_pre"][i],
                                     p["norm_post"][i], p["w_in_sb"][j], p["w_out_sb"][j], tiles)
    return x, kv_stacks[0], kv_stacks[1], jnp.stack(new_re), jnp.stack(new_im)


def kernel(x_prompt, x_sample, cache_sb_k, cache_sb_v, state_ssm_re, state_ssm_im, norm_pre, norm_post, w_in_ssm, ssm_a_re, ssm_a_im, ssm_log_step, ssm_b_re, ssm_b_im, ssm_c_re, ssm_c_im, ssm_d, w_glu, w_out_ssm, w_in_sb, w_out_sb):
    p = dict(norm_pre=norm_pre, norm_post=norm_post, w_in_ssm=w_in_ssm, ssm_a_re=ssm_a_re,
             ssm_a_im=ssm_a_im, ssm_log_step=ssm_log_step, ssm_b_re=ssm_b_re, ssm_b_im=ssm_b_im,
             ssm_c_re=ssm_c_re, ssm_c_im=ssm_c_im, ssm_d=ssm_d, w_glu=w_glu,
             w_out_ssm=w_out_ssm, w_in_sb=w_in_sb, w_out_sb=w_out_sb)
    nb = x_prompt.shape[0]
    n_ssm = state_ssm_re.shape[0]
    zeros = jnp.zeros((n_ssm, nb) + state_ssm_re.shape[2:], F32)
    y_p, k_p, v_p, re_p, im_p = _run_trunk(x_prompt, zeros, zeros, None, None, p)
    y_s, k_s, v_s, re_s, im_s = _run_trunk(x_sample, state_ssm_re, state_ssm_im,
                                           cache_sb_k, cache_sb_v, p)
    return (y_p, y_s, k_p, v_p, re_p, im_p, k_s, v_s, re_s, im_s)
```

```python
import functools
import math
import typing

import jax
import jax.numpy as jnp
from jax import lax
from jax.experimental import pallas as pl
from jax.experimental.pallas import tpu as pltpu

F32 = jnp.float32
BF16 = jnp.bfloat16

RMS_EPS = 1e-6
SSM_GROUP = 16
SSM_STATE = 64
HEAD_DIM = 128

V7X_LANES = 128
V7X_VMEM_BYTES = 64 * 1024 * 1024

LOG2_E = math.log2(math.e)
SB_SKIP_CARRY = 104.0 * LOG2_E
SB_STATIC_BLOCKS = 2

SSM_CH_TILE = V7X_LANES
SSM_G_TILE = SSM_CH_TILE // SSM_GROUP
SSM_S_TILE = SSM_G_TILE * SSM_STATE
SCAN_COLS = V7X_LANES


def _compiler_params(semantics, vmem_mib):
    assert vmem_mib * 1024 * 1024 < V7X_VMEM_BYTES
    return pltpu.CompilerParams(dimension_semantics=semantics,
                                vmem_limit_bytes=vmem_mib * 1024 * 1024)


def _resident(shape):
    zeros = (0,) * len(shape)
    return pl.BlockSpec(shape, lambda *_: zeros, pipeline_mode=pl.Buffered(1))


def _rms(x, g):
    return x * lax.rsqrt(jnp.mean(x * x, axis=-1, keepdims=True) + RMS_EPS) * g


def _silu(x):
    return x * jax.nn.sigmoid(x)


def _ssm_in_kernel(x_ref, g_ref, wu_ref, wg_ref, u_ref, gate_ref):
    xn = _rms(x_ref[...], g_ref[...]).astype(BF16)
    u_ref[...] = jnp.dot(xn, wu_ref[...], preferred_element_type=F32)
    gate_ref[...] = jnp.dot(xn, wg_ref[...], preferred_element_type=F32)


def _ssm_in_proj(x, g, wu, wg, tc):
    n, t, d = x.shape
    w = wu.shape[1]
    return pl.pallas_call(
        _ssm_in_kernel,
        grid=(n, t // tc),
        in_specs=[pl.BlockSpec((None, tc, d), lambda i, c: (i, c, 0)),
                  _resident((1, d)), _resident((d, w)), _resident((d, w))],
        out_specs=[pl.BlockSpec((tc, w), lambda i, c: (c, i)),
                   pl.BlockSpec((None, tc, w), lambda i, c: (i, c, 0))],
        out_shape=[jax.ShapeDtypeStruct((t, n * w), F32),
                   jax.ShapeDtypeStruct((n, t, w), F32)],
        compiler_params=_compiler_params(("parallel", "parallel"), 48),
        name="ssm_in_proj",
    )(x, g, wu, wg)


def _ssm_mixer_kernel(u_ref, h0r_ref, h0i_ref, bm_ref, cm_ref, lbr_ref, lbi_ref, d_ref,
                      y_ref, htr_ref, hti_ref, sr_ref, si_ref, bu_ref, hs_ref, *, tc, n):
    c = pl.program_id(1)

    @pl.when(c == 0)
    def _():
        sr_ref[...] = h0r_ref[...]
        si_ref[...] = h0i_ref[...]

    u2 = u_ref[...].reshape(tc * n, SSM_CH_TILE)
    u2b = u2.astype(BF16)
    y = d_ref[...] * u2

    for q in range(SSM_S_TILE // SCAN_COLS):
        st_cols = slice(q * SCAN_COLS, (q + 1) * SCAN_COLS)
        both = slice(2 * q * SCAN_COLS, (2 * q + 2) * SCAN_COLS)
        slot = q % bu_ref.shape[0]
        bu_ref[slot] = jnp.dot(u2b, bm_ref[:, both], preferred_element_type=F32)
        lr = jnp.broadcast_to(lbr_ref[:, st_cols], (n, SCAN_COLS))
        li = jnp.broadcast_to(lbi_ref[:, st_cols], (n, SCAN_COLS))
        hr = sr_ref[:, st_cols]
        hi = si_ref[:, st_cols]
        for t in range(tc):
            rows = slice(t * n, (t + 1) * n)
            hr, hi = (lr * hr - li * hi + bu_ref[slot, rows, :SCAN_COLS],
                      lr * hi + li * hr + bu_ref[slot, rows, SCAN_COLS:])
            hs_ref[slot, rows, :SCAN_COLS] = hr.astype(hs_ref.dtype)
            hs_ref[slot, rows, SCAN_COLS:] = hi.astype(hs_ref.dtype)
        sr_ref[:, st_cols] = hr
        si_ref[:, st_cols] = hi
        y = y + jnp.dot(hs_ref[slot].astype(BF16), cm_ref[both, :],
                        preferred_element_type=F32)

    y_ref[...] = y.reshape(tc, n, SSM_CH_TILE)

    @pl.when(c == pl.num_programs(1) - 1)
    def _():
        htr_ref[...] = sr_ref[...]
        hti_ref[...] = si_ref[...]


def _ssm_mixer(u_tm, h0r, h0i, bm, cm, lbr, lbi, dskip, tc):
    t, n, w = u_tm.shape
    tiles = w // SSM_CH_TILE
    hs_dtype = BF16 if n % 16 == 0 else F32
    kernel = functools.partial(_ssm_mixer_kernel, tc=tc, n=n)
    return pl.pallas_call(
        kernel,
        grid=(tiles, t // tc),
        in_specs=[pl.BlockSpec((tc, n, SSM_CH_TILE), lambda j, c: (c, 0, j)),
                  pl.BlockSpec((n, SSM_S_TILE), lambda j, c: (0, j)),
                  pl.BlockSpec((n, SSM_S_TILE), lambda j, c: (0, j)),
                  pl.BlockSpec((None, SSM_CH_TILE, 2 * SSM_S_TILE), lambda j, c: (j, 0, 0)),
                  pl.BlockSpec((None, 2 * SSM_S_TILE, SSM_CH_TILE), lambda j, c: (j, 0, 0)),
                  pl.BlockSpec((None, 1, SSM_S_TILE), lambda j, c: (j, 0, 0)),
                  pl.BlockSpec((None, 1, SSM_S_TILE), lambda j, c: (j, 0, 0)),
                  pl.BlockSpec((1, SSM_CH_TILE), lambda j, c: (0, j))],
        out_specs=[pl.BlockSpec((tc, n, SSM_CH_TILE), lambda j, c: (c, 0, j)),
                   pl.BlockSpec((n, SSM_S_TILE), lambda j, c: (0, j)),
                   pl.BlockSpec((n, SSM_S_TILE), lambda j, c: (0, j))],
        out_shape=[jax.ShapeDtypeStruct((t, n, w), F32),
                   jax.ShapeDtypeStruct(h0r.shape, F32),
                   jax.ShapeDtypeStruct(h0i.shape, F32)],
        scratch_shapes=[pltpu.VMEM((n, SSM_S_TILE), F32),
                        pltpu.VMEM((n, SSM_S_TILE), F32),
                        pltpu.VMEM((2, tc * n, 2 * SCAN_COLS), F32),
                        pltpu.VMEM((2, tc * n, 2 * SCAN_COLS), hs_dtype)],
        compiler_params=_compiler_params(("parallel", "arbitrary"), 48),
        name="ssm_mixer",
    )(u_tm, h0r, h0i, bm, cm, lbr, lbi, dskip)


def _ssm_out_kernel(y_ref, gate_ref, x_ref, wglu_ref, wout_ref, g_ref, o_ref):
    y = jax.nn.gelu(y_ref[...])
    z = jnp.dot(y.astype(BF16), wglu_ref[...], preferred_element_type=F32)
    y = y * jax.nn.sigmoid(z)
    a = (y * _silu(gate_ref[...])).astype(BF16)
    o = jnp.dot(a, wout_ref[...], preferred_element_type=F32)
    o_ref[...] = x_ref[...] + _rms(o, g_ref[...])


def _ssm_out_proj(y_tm, gate, x, wglu, wout, g, tc):
    n, t, d = x.shape
    w = gate.shape[2]
    return pl.pallas_call(
        _ssm_out_kernel,
        grid=(n, t // tc),
        in_specs=[pl.BlockSpec((tc, w), lambda i, c: (c, i)),
                  pl.BlockSpec((None, tc, w), lambda i, c: (i, c, 0)),
                  pl.BlockSpec((None, tc, d), lambda i, c: (i, c, 0)),
                  _resident((w, w)), _resident((w, d)), _resident((1, d))],
        out_specs=pl.BlockSpec((None, tc, d), lambda i, c: (i, c, 0)),
        out_shape=jax.ShapeDtypeStruct((n, t, d), F32),
        compiler_params=_compiler_params(("parallel", "parallel"), 48),
        name="ssm_out_proj",
    )(y_tm, gate, x, wglu, wout, g)


def _ssm_weights(a_re, a_im, log_step, b_re, b_im, c_re, c_im):
    ar = a_re.astype(F32)
    ai = a_im.astype(F32)
    dt = jnp.exp(log_step.astype(F32))[:, None]
    mag = jnp.exp(ar * dt)
    ph = ai * dt
    lb_re = mag * jnp.cos(ph)
    lb_im = mag * jnp.sin(ph)
    nr = lb_re - 1.0
    ni = lb_im
    den = ar * ar + ai * ai
    fac_re = (nr * ar + ni * ai) / den
    fac_im = (ni * ar - nr * ai) / den
    br = b_re.astype(F32)
    bi = b_im.astype(F32)
    bbar_re = fac_re[..., None] * br - fac_im[..., None] * bi
    bbar_im = fac_re[..., None] * bi + fac_im[..., None] * br

    groups = a_re.shape[0]
    tiles = groups // SSM_G_TILE
    passes = SSM_S_TILE // SCAN_COLS
    gp = SSM_G_TILE // passes
    eye = jnp.eye(SSM_G_TILE, dtype=BF16).reshape(SSM_G_TILE, passes, gp)

    bb = jnp.stack([bbar_re, bbar_im], axis=1).astype(BF16)
    bb = bb.reshape(tiles, SSM_G_TILE, 2, SSM_STATE, SSM_GROUP)
    bm = jnp.einsum("tgcph,gqk->tghqckp", bb, eye).reshape(tiles, SSM_CH_TILE, 2 * SSM_S_TILE)
    cc = jnp.stack([c_re.astype(F32), -c_im.astype(F32)], axis=1).astype(BF16)
    cc = cc.reshape(tiles, SSM_G_TILE, 2, SSM_GROUP, SSM_STATE)
    cm = jnp.einsum("tgchp,gqk->tqckpgh", cc, eye).reshape(tiles, 2 * SSM_S_TILE, SSM_CH_TILE)
    lbr = lb_re.reshape(tiles, 1, SSM_S_TILE)
    lbi = lb_im.reshape(tiles, 1, SSM_S_TILE)
    return bm, cm, lbr, lbi


def _ssm_layer(x, h0r, h0i, g_pre, g_post, w_in, a_re, a_im, log_step, b_re, b_im,
               c_re, c_im, d_skip, w_glu, w_out, tiles):
    n, t, _ = x.shape
    w = w_glu.shape[0]
    wu = w_in[:, :w].astype(BF16)
    wg = w_in[:, w:].astype(BF16)
    bm, cm, lbr, lbi = _ssm_weights(a_re, a_im, log_step, b_re, b_im, c_re, c_im)
    u_tm, gate = _ssm_in_proj(x, g_pre[None, :], wu, wg, tiles.rows)
    y_tm, htr, hti = _ssm_mixer(u_tm.reshape(t, n, w), h0r.reshape(n, -1), h0i.reshape(n, -1),
                                bm, cm, lbr, lbi, d_skip[None, :], tiles.scan)
    x = _ssm_out_proj(y_tm.reshape(t, n * w), gate, x, w_glu.astype(BF16), w_out.astype(BF16),
                      g_post[None, :], tiles.rows)
    return x, htr.reshape(h0r.shape), hti.reshape(h0i.shape)


def _sb_in_kernel(x_ref, g_ref, wq_ref, wk_ref, wv_ref, wg_ref, *rest):
    q_ref, k_ref, v_ref, gate_ref = rest[-4:]
    xn = _rms(x_ref[...], g_ref[...]).astype(BF16)
    for w_ref, o_ref, act in ((wq_ref, q_ref, None), (wk_ref, k_ref, None), (wv_ref, v_ref, None),
                              (wg_ref, gate_ref, _silu)):
        p = jnp.dot(xn, w_ref[...], preferred_element_type=F32)
        if act is not None:
            p = act(p)
        for h in range(o_ref.shape[0]):
            o_ref[h] = p[:, h * HEAD_DIM:(h + 1) * HEAD_DIM]


def _sb_in_proj(x, g, wq, wk, wv, wg, tc, layer, n_layers, kv_stacks):
    n, t, d = x.shape
    w = wq.shape[1]
    heads = w // HEAD_DIM
    q_spec = pl.BlockSpec((None, heads, tc, HEAD_DIM), lambda i, c: (i, 0, c, 0))
    kv_spec = pl.BlockSpec((None, None, heads, tc, HEAD_DIM), lambda i, c: (layer, i, 0, c, 0))
    kv_shape = jax.ShapeDtypeStruct((n_layers, n, heads, t, HEAD_DIM), F32)
    in_specs = [pl.BlockSpec((None, tc, d), lambda i, c: (i, c, 0)),
                _resident((1, d)), _resident((d, w)), _resident((d, w)),
                _resident((d, w)), _resident((d, w))]
    args = [x, g, wq, wk, wv, wg]
    aliases = {}
    if kv_stacks is not None:
        aliases = {len(args): 1, len(args) + 1: 2}
        in_specs += [pl.BlockSpec(memory_space=pl.ANY)] * 2
        args += list(kv_stacks)
    return pl.pallas_call(
        _sb_in_kernel,
        grid=(n, t // tc),
        in_specs=in_specs,
        out_specs=[q_spec, kv_spec, kv_spec, q_spec],
        out_shape=[jax.ShapeDtypeStruct((n, heads, t, HEAD_DIM), F32), kv_shape, kv_shape,
                   jax.ShapeDtypeStruct((n, heads, t, HEAD_DIM), F32)],
        input_output_aliases=aliases,
        compiler_params=_compiler_params(("parallel", "parallel"), 56),
        name="sb_in_proj",
    )(*args)


def _strict_lower_ones(size):
    j = lax.broadcasted_iota(jnp.int32, (2 * size, size), 0)
    s = lax.broadcasted_iota(jnp.int32, (2 * size, size), 1)
    return jnp.where(jnp.where(j >= size, j - size, j) > s, 1.0, 0.0).astype(BF16)


def _sb_tile(q, kb, vb, tri, carry, mask):
    z = lax.dot_general(q, kb, (((1,), (1,)), ((), ())), preferred_element_type=F32)
    neg_abs = lax.bitcast_convert_type(
        lax.bitcast_convert_type(z, jnp.uint32) | jnp.uint32(0x80000000), F32)
    sp = jnp.maximum(z, 0.0) + jnp.log(1.0 + jnp.exp2(neg_abs)) * LOG2_E
    if mask is not None:
        sp = jnp.where(mask, sp, 0.0)
    hi = sp.astype(BF16)
    lo = (sp - hi.astype(F32)).astype(BF16)
    rest = jnp.dot(jnp.concatenate([hi, lo], axis=1), tri, preferred_element_type=F32)
    arg = z - sp - rest
    total = jnp.sum(sp, axis=-1, keepdims=True)
    if carry is not None:
        arg = arg - carry
        total = total + carry
    w = jnp.exp2(arg)
    if mask is not None:
        w = jnp.where(mask, w, 0.0)
    return total, jnp.dot(w.astype(BF16), vb, preferred_element_type=F32)


def _sb_tail(q, carry, rows, k_ref, v_ref, n_blocks, blk, tri, acc_ref):
    def body(j, carry):
        off = pl.multiple_of((n_blocks - 1 - j) * blk, blk)
        kb = k_ref[pl.ds(off, blk), :].astype(BF16)
        vb = v_ref[pl.ds(off, blk), :].astype(BF16)
        carry, part = _sb_tile(q, kb, vb, tri, carry, None)
        acc_ref[rows, :] += part
        return carry

    return lax.fori_loop(0, n_blocks, body, carry)


def _sb_attn_kernel(*refs, bq, bk, past, emit_state):
    if emit_state:
        *refs, acc_out_ref, carry_out_ref, acc_ref = refs
        refs.append(acc_ref)
    if past:
        q_ref, kn_ref, vn_ref, kp_ref, vp_ref, gate_ref, o_ref, acc_ref = refs
    else:
        q_ref, kn_ref, vn_ref, gate_ref, o_ref, acc_ref = refs
    t = q_ref.shape[0]
    nq = t // bq
    n_past = past // bk
    tri_new = _strict_lower_ones(bq)
    tri_past = tri_new if bk == bq else _strict_lower_ones(bk)
    row = lax.broadcasted_iota(jnp.int32, (bq, bq), 0)
    col = lax.broadcasted_iota(jnp.int32, (bq, bq), 1)
    earlier = col < row

    casts = {}

    def kv_block(k_ref, v_ref, b, blk):
        key = (id(k_ref), b)
        if key not in casts:
            rows = slice(b * blk, (b + 1) * blk)
            casts[key] = (k_ref[rows, :].astype(BF16), v_ref[rows, :].astype(BF16))
        return casts[key]

    tails = []
    for i in range(nq):
        rows = slice(i * bq, (i + 1) * bq)
        q = (q_ref[rows, :] * (HEAD_DIM ** -0.5 * LOG2_E)).astype(BF16)
        new_left, past_left = i + 1, n_past
        carry = acc = None
        for _ in range(SB_STATIC_BLOCKS):
            if new_left:
                new_left -= 1
                kb, vb = kv_block(kn_ref, vn_ref, new_left, bq)
                carry, part = _sb_tile(q, kb, vb, tri_new, carry, earlier if new_left == i else None)
            elif past_left:
                past_left -= 1
                kb, vb = kv_block(kp_ref, vp_ref, past_left, bk)
                carry, part = _sb_tile(q, kb, vb, tri_past, carry, None)
            else:
                break
            acc = part if acc is None else acc + part
        acc_ref[rows, :] = acc
        if new_left or past_left:
            tails.append((rows, q, carry, new_left, past_left))
        if emit_state:
            acc_out_ref[rows, :] = acc
            carry_out_ref[rows, :] = carry

    assert not (emit_state and tails)
    if tails:
        lowest = functools.reduce(jnp.minimum, [c for _, _, c, _, _ in tails])

        @pl.when(jnp.min(lowest) < SB_SKIP_CARRY)
        def _():
            for rows, q, carry, new_left, past_left in tails:
                @pl.when(jnp.min(carry) < SB_SKIP_CARRY)
                def _(rows=rows, q=q, carry=carry, new_left=new_left, past_left=past_left):
                    if new_left:
                        carry = _sb_tail(q, carry, rows, kn_ref, vn_ref, new_left, bq, tri_new, acc_ref)
                    if past_left:
                        _sb_tail(q, carry, rows, kp_ref, vp_ref, past_left, bk, tri_past, acc_ref)

    o_ref[...] = (acc_ref[...] * gate_ref[...]).astype(BF16)


def _sb_rest_kernel(q_ref, acc_in_ref, carry_ref, kp_ref, vp_ref, gate_ref, o_ref, acc_ref, *,
                    bk, n_blocks):
    acc_ref[...] = acc_in_ref[...]
    carry = carry_ref[...]

    @pl.when(jnp.min(carry) < SB_SKIP_CARRY)
    def _():
        q = (q_ref[...] * (HEAD_DIM ** -0.5 * LOG2_E)).astype(BF16)
        _sb_tail(q, carry, slice(None), kp_ref, vp_ref, n_blocks, bk, _strict_lower_ones(bk), acc_ref)

    o_ref[...] = (acc_ref[...] * gate_ref[...]).astype(BF16)


def _sb_attention_rest(q, acc, carry, k_past, v_past, gate_act, layer, bk, n_blocks):
    n, heads, t, dh = q.shape
    past = k_past.shape[3]
    q_spec = pl.BlockSpec((None, None, t, dh), lambda i, h: (i, h, 0, 0))
    carry_spec = pl.BlockSpec((None, None, t, 1), lambda i, h: (i, h, 0, 0))
    past_spec = pl.BlockSpec((None, None, None, past, dh), lambda i, h: (layer, i, h, 0, 0))
    return pl.pallas_call(
        functools.partial(_sb_rest_kernel, bk=bk, n_blocks=n_blocks),
        grid=(n, heads),
        in_specs=[q_spec, q_spec, carry_spec, past_spec, past_spec, q_spec],
        out_specs=pl.BlockSpec((None, t, dh), lambda i, h: (i, 0, h)),
        out_shape=jax.ShapeDtypeStruct((n, t, heads * dh), BF16),
        scratch_shapes=[pltpu.VMEM((t, dh), F32)],
        compiler_params=_compiler_params(("parallel", "parallel"), 48),
        name="sb_attention_rest",
    )(q, acc, carry, k_past, v_past, gate_act)


def _sb_attention(q, k, v, k_past, v_past, gate_act, layer, bq, bk):
    n, heads, t, dh = q.shape
    past = 0 if k_past is None else k_past.shape[3]
    assert t % bq == 0 and past % bk == 0 and (not past or t == bq)
    n_past = past // bk
    q_spec = pl.BlockSpec((None, None, t, dh), lambda i, h: (i, h, 0, 0))
    kv_spec = pl.BlockSpec((None, None, None, t, dh), lambda i, h: (layer, i, h, 0, 0))
    o_spec = pl.BlockSpec((None, t, dh), lambda i, h: (i, 0, h))
    o_shape = jax.ShapeDtypeStruct((n, t, heads * dh), BF16)
    in_specs = [q_spec, kv_spec, kv_spec]
    args = [q, k, v]
    if past:
        last_spec = pl.BlockSpec((None, None, None, bk, dh), lambda i, h: (layer, i, h, n_past - 1, 0))
        in_specs += [last_spec, last_spec]
        args += [k_past, v_past]
    in_specs.append(q_spec)
    args.append(gate_act)
    out_specs, out_shape = [o_spec], [o_shape]
    if past:
        out_specs += [q_spec, pl.BlockSpec((None, None, t, 1), lambda i, h: (i, h, 0, 0))]
        out_shape += [jax.ShapeDtypeStruct((n, heads, t, dh), F32),
                      jax.ShapeDtypeStruct((n, heads, t, 1), F32)]
    kernel = functools.partial(_sb_attn_kernel, bq=bq, bk=bk, past=min(past, bk),
                               emit_state=bool(past))
    out = pl.pallas_call(
        kernel,
        grid=(n, heads),
        in_specs=in_specs,
        out_specs=out_specs,
        out_shape=out_shape,
        scratch_shapes=[pltpu.VMEM((t, dh), F32)],
        compiler_params=_compiler_params(("parallel", "parallel"), 48),
        name="sb_attention",
    )(*args)
    if n_past <= 1:
        return out[0]
    o, acc, carry = out
    return lax.cond(jnp.min(carry) < SB_SKIP_CARRY,
                    lambda: _sb_attention_rest(q, acc, carry, k_past, v_past, gate_act, layer, bk,
                                               n_past - 1),
                    lambda: o)


def _sb_out_kernel(a_ref, x_ref, wout_ref, g_ref, o_ref):
    o = jnp.dot(a_ref[...], wout_ref[...], preferred_element_type=F32)
    o_ref[...] = x_ref[...] + _rms(o, g_ref[...])


def _sb_out_proj(a, x, wout, g, tc):
    n, t, d = x.shape
    w = a.shape[2]
    return pl.pallas_call(
        _sb_out_kernel,
        grid=(n, t // tc),
        in_specs=[pl.BlockSpec((None, tc, w), lambda i, c: (i, c, 0)),
                  pl.BlockSpec((None, tc, d), lambda i, c: (i, c, 0)),
                  _resident((w, d)), _resident((1, d))],
        out_specs=pl.BlockSpec((None, tc, d), lambda i, c: (i, c, 0)),
        out_shape=jax.ShapeDtypeStruct((n, t, d), F32),
        compiler_params=_compiler_params(("parallel", "parallel"), 48),
        name="sb_out_proj",
    )(a, x, wout, g)


def _sb_layer(x, k_past, v_past, kv_stacks, layer, n_layers, g_pre, g_post, w_in, w_out,
              tiles):
    w = w_out.shape[0]
    wq, wk, wv, wg = (w_in[:, i * w:(i + 1) * w].astype(BF16) for i in range(4))
    q, k, v, gate_act = _sb_in_proj(x, g_pre[None, :], wq, wk, wv, wg, tiles.sb_in_rows, layer,
                                    n_layers, kv_stacks)
    a = _sb_attention(q, k, v, k_past, v_past, gate_act, layer, tiles.queries, tiles.keys)
    x = _sb_out_proj(a, x, w_out.astype(BF16), g_post[None, :], tiles.rows)
    return x, (k, v)


class _TrunkTiles(typing.NamedTuple):
    rows: int
    sb_in_rows: int
    scan: int
    queries: int
    keys: int


def _trunk_tiles(t):
    return _TrunkTiles(rows=min(512, t), sb_in_rows=min(256, t), scan=min(256, t),
                       queries=min(256, t), keys=256)


def _run_trunk(x, h0_re, h0_im, k_past, v_past, p):
    depth = p["norm_pre"].shape[0]
    n_sb = p["w_in_sb"].shape[0]
    tiles = _trunk_tiles(x.shape[1])
    kv_stacks = None
    new_re, new_im = [], []
    for i in range(depth):
        j = i // 2
        if i % 2 == 0:
            x, hr, hi = _ssm_layer(
                x, h0_re[j], h0_im[j], p["norm_pre"][i], p["norm_post"][i], p["w_in_ssm"][j],
                p["ssm_a_re"][j], p["ssm_a_im"][j], p["ssm_log_step"][j], p["ssm_b_re"][j],
                p["ssm_b_im"][j], p["ssm_c_re"][j], p["ssm_c_im"][j], p["ssm_d"][j],
                p["w_glu"][j], p["w_out_ssm"][j], tiles)
            new_re.append(hr)
            new_im.append(hi)
        else:
            x, kv_stacks = _sb_layer(x, k_past, v_past, kv_stacks, j, n_sb, p["norm_pre"][i],
                                     p["norm_post"][i], p["w_in_sb"][j], p["w_out_sb"][j], tiles)
    return x, kv_stacks[0], kv_stacks[1], jnp.stack(new_re), jnp.stack(new_im)


def kernel(x_prompt, x_sample, cache_sb_k, cache_sb_v, state_ssm_re, state_ssm_im, norm_pre, norm_post, w_in_ssm, ssm_a_re, ssm_a_im, ssm_log_step, ssm_b_re, ssm_b_im, ssm_c_re, ssm_c_im, ssm_d, w_glu, w_out_ssm, w_in_sb, w_out_sb):
    p = dict(norm_pre=norm_pre, norm_post=norm_post, w_in_ssm=w_in_ssm, ssm_a_re=ssm_a_re,
             ssm_a_im=ssm_a_im, ssm_log_step=ssm_log_step, ssm_b_re=ssm_b_re, ssm_b_im=ssm_b_im,
             ssm_c_re=ssm_c_re, ssm_c_im=ssm_c_im, ssm_d=ssm_d, w_glu=w_glu,
             w_out_ssm=w_out_ssm, w_in_sb=w_in_sb, w_out_sb=w_out_sb)
    nb = x_prompt.shape[0]
    n_ssm = state_ssm_re.shape[0]
    zeros = jnp.zeros((n_ssm, nb) + state_ssm_re.shape[2:], F32)
    y_p, k_p, v_p, re_p, im_p = _run_trunk(x_prompt, zeros, zeros, None, None, p)
    y_s, k_s, v_s, re_s, im_s = _run_trunk(x_sample, state_ssm_re, state_ssm_im,
                                           cache_sb_k, cache_sb_v, p)
    return (y_p, y_s, k_p, v_p, re_p, im_p, k_s, v_s, re_s, im_s)
```

```python
import functools
import math
import typing

import jax
import jax.numpy as jnp
from jax import lax
from jax.experimental import pallas as pl
from jax.experimental.pallas import tpu as pltpu

F32 = jnp.float32
BF16 = jnp.bfloat16

RMS_EPS = 1e-6
SSM_GROUP = 16
SSM_STATE = 64
HEAD_DIM = 128

V7X_LANES = 128
V7X_VMEM_BYTES = 64 * 1024 * 1024

LOG2_E = math.log2(math.e)
SB_SKIP_CARRY = 104.0 * LOG2_E
SB_STATIC_BLOCKS = 2

SSM_CH_TILE = V7X_LANES
SSM_G_TILE = SSM_CH_TILE // SSM_GROUP
SSM_S_TILE = SSM_G_TILE * SSM_STATE
SCAN_COLS = V7X_LANES


def _compiler_params(semantics, vmem_mib):
    assert vmem_mib * 1024 * 1024 < V7X_VMEM_BYTES
    return pltpu.CompilerParams(dimension_semantics=semantics,
                                vmem_limit_bytes=vmem_mib * 1024 * 1024)


def _resident(shape):
    zeros = (0,) * len(shape)
    return pl.BlockSpec(shape, lambda *_: zeros, pipeline_mode=pl.Buffered(1))


def _rms(x, g):
    return x * lax.rsqrt(jnp.mean(x * x, axis=-1, keepdims=True) + RMS_EPS) * g


def _silu(x):
    return x * jax.nn.sigmoid(x)


def _ssm_in_kernel(x_ref, g_ref, wu_ref, wg_ref, u_ref, gate_ref):
    xn = _rms(x_ref[...], g_ref[...]).astype(BF16)
    u_ref[...] = jnp.dot(xn, wu_ref[...], preferred_element_type=F32)
    gate_ref[...] = jnp.dot(xn, wg_ref[...], preferred_element_type=F32)


def _ssm_in_proj(x, g, wu, wg, tc):
    n, t, d = x.shape
    w = wu.shape[1]
    return pl.pallas_call(
        _ssm_in_kernel,
        grid=(n, t // tc),
        in_specs=[pl.BlockSpec((None, tc, d), lambda i, c: (i, c, 0)),
                  _resident((1, d)), _resident((d, w)), _resident((d, w))],
        out_specs=[pl.BlockSpec((tc, w), lambda i, c: (c, i)),
                   pl.BlockSpec((None, tc, w), lambda i, c: (i, c, 0))],
        out_shape=[jax.ShapeDtypeStruct((t, n * w), F32),
                   jax.ShapeDtypeStruct((n, t, w), F32)],
        compiler_params=_compiler_params(("parallel", "parallel"), 48),
        name="ssm_in_proj",
    )(x, g, wu, wg)


def _ssm_mixer_kernel(u_ref, h0r_ref, h0i_ref, bm_ref, cm_ref, lbr_ref, lbi_ref, d_ref,
                      y_ref, htr_ref, hti_ref, sr_ref, si_ref, bu_ref, hs_ref, *, tc, n):
    c = pl.program_id(1)

    @pl.when(c == 0)
    def _():
        sr_ref[...] = h0r_ref[...]
        si_ref[...] = h0i_ref[...]

    u2 = u_ref[...].reshape(tc * n, SSM_CH_TILE)
    u2b = u2.astype(BF16)
    y = d_ref[...] * u2

    gp = SCAN_COLS // SSM_STATE
    b_group = lax.broadcasted_iota(jnp.int32, bm_ref.shape, 0) // SSM_GROUP
    b_slot = lax.broadcasted_iota(jnp.int32, bm_ref.shape, 1) % SCAN_COLS // SSM_STATE
    c_group = lax.broadcasted_iota(jnp.int32, cm_ref.shape, 1) // SSM_GROUP
    c_slot = lax.broadcasted_iota(jnp.int32, cm_ref.shape, 0) % SCAN_COLS // SSM_STATE
    zero = jnp.zeros((), BF16)

    for q in range(SSM_S_TILE // SCAN_COLS):
        st_cols = slice(q * SCAN_COLS, (q + 1) * SCAN_COLS)
        slot = q % bu_ref.shape[0]
        bm_q = jnp.where(b_group == q * gp + b_slot, bm_ref[...], zero)
        cm_q = jnp.where(c_group == q * gp + c_slot, cm_ref[...], zero)
        bu_ref[slot] = jnp.dot(u2b, bm_q, preferred_element_type=F32)
        lr = jnp.broadcast_to(lbr_ref[:, st_cols], (n, SCAN_COLS))
        li = jnp.broadcast_to(lbi_ref[:, st_cols], (n, SCAN_COLS))
        hr = sr_ref[:, st_cols]
        hi = si_ref[:, st_cols]
        for t in range(tc):
            rows = slice(t * n, (t + 1) * n)
            hr, hi = (lr * hr - li * hi + bu_ref[slot, rows, :SCAN_COLS],
                      lr * hi + li * hr + bu_ref[slot, rows, SCAN_COLS:])
            hs_ref[slot, rows, :SCAN_COLS] = hr.astype(hs_ref.dtype)
            hs_ref[slot, rows, SCAN_COLS:] = hi.astype(hs_ref.dtype)
        sr_ref[:, st_cols] = hr
        si_ref[:, st_cols] = hi
        y = y + jnp.dot(hs_ref[slot].astype(BF16), cm_q, preferred_element_type=F32)

    y_ref[...] = y.reshape(tc, n, SSM_CH_TILE)

    @pl.when(c == pl.num_programs(1) - 1)
    def _():
        htr_ref[...] = sr_ref[...]
        hti_ref[...] = si_ref[...]


def _ssm_mixer(u_tm, h0r, h0i, bm, cm, lbr, lbi, dskip, tc):
    t, n, w = u_tm.shape
    tiles = w // SSM_CH_TILE
    hs_dtype = BF16 if n % 16 == 0 else F32
    kernel = functools.partial(_ssm_mixer_kernel, tc=tc, n=n)
    return pl.pallas_call(
        kernel,
        grid=(tiles, t // tc),
        in_specs=[pl.BlockSpec((tc, n, SSM_CH_TILE), lambda j, c: (c, 0, j)),
                  pl.BlockSpec((n, SSM_S_TILE), lambda j, c: (0, j)),
                  pl.BlockSpec((n, SSM_S_TILE), lambda j, c: (0, j)),
                  pl.BlockSpec((None, SSM_CH_TILE, 2 * SCAN_COLS), lambda j, c: (j, 0, 0)),
                  pl.BlockSpec((None, 2 * SCAN_COLS, SSM_CH_TILE), lambda j, c: (j, 0, 0)),
                  pl.BlockSpec((None, 1, SSM_S_TILE), lambda j, c: (j, 0, 0)),
                  pl.BlockSpec((None, 1, SSM_S_TILE), lambda j, c: (j, 0, 0)),
                  pl.BlockSpec((1, SSM_CH_TILE), lambda j, c: (0, j))],
        out_specs=[pl.BlockSpec((tc, n, SSM_CH_TILE), lambda j, c: (c, 0, j)),
                   pl.BlockSpec((n, SSM_S_TILE), lambda j, c: (0, j)),
                   pl.BlockSpec((n, SSM_S_TILE), lambda j, c: (0, j))],
        out_shape=[jax.ShapeDtypeStruct((t, n, w), F32),
                   jax.ShapeDtypeStruct(h0r.shape, F32),
                   jax.ShapeDtypeStruct(h0i.shape, F32)],
        scratch_shapes=[pltpu.VMEM((n, SSM_S_TILE), F32),
                        pltpu.VMEM((n, SSM_S_TILE), F32),
                        pltpu.VMEM((2, tc * n, 2 * SCAN_COLS), F32),
                        pltpu.VMEM((2, tc * n, 2 * SCAN_COLS), hs_dtype)],
        compiler_params=_compiler_params(("parallel", "arbitrary"), 48),
        name="ssm_mixer",
    )(u_tm, h0r, h0i, bm, cm, lbr, lbi, dskip)


def _ssm_out_kernel(y_ref, gate_ref, x_ref, wglu_ref, wout_ref, g_ref, o_ref):
    y = jax.nn.gelu(y_ref[...])
    z = jnp.dot(y.astype(BF16), wglu_ref[...], preferred_element_type=F32)
    y = y * jax.nn.sigmoid(z)
    a = (y * _silu(gate_ref[...])).astype(BF16)
    o = jnp.dot(a, wout_ref[...], preferred_element_type=F32)
    o_ref[...] = x_ref[...] + _rms(o, g_ref[...])


def _ssm_out_proj(y_tm, gate, x, wglu, wout, g, tc):
    n, t, d = x.shape
    w = gate.shape[2]
    return pl.pallas_call(
        _ssm_out_kernel,
        grid=(n, t // tc),
        in_specs=[pl.BlockSpec((tc, w), lambda i, c: (c, i)),
                  pl.BlockSpec((None, tc, w), lambda i, c: (i, c, 0)),
                  pl.BlockSpec((None, tc, d), lambda i, c: (i, c, 0)),
                  _resident((w, w)), _resident((w, d)), _resident((1, d))],
        out_specs=pl.BlockSpec((None, tc, d), lambda i, c: (i, c, 0)),
        out_shape=jax.ShapeDtypeStruct((n, t, d), F32),
        compiler_params=_compiler_params(("parallel", "parallel"), 48),
        name="ssm_out_proj",
    )(y_tm, gate, x, wglu, wout, g)


def _ssm_weights(a_re, a_im, log_step, b_re, b_im, c_re, c_im):
    ar = a_re.astype(F32)
    ai = a_im.astype(F32)
    dt = jnp.exp(log_step.astype(F32))[:, None]
    mag = jnp.exp(ar * dt)
    ph = ai * dt
    lb_re = mag * jnp.cos(ph)
    lb_im = mag * jnp.sin(ph)
    nr = lb_re - 1.0
    ni = lb_im
    den = ar * ar + ai * ai
    fac_re = (nr * ar + ni * ai) / den
    fac_im = (ni * ar - nr * ai) / den
    br = b_re.astype(F32)
    bi = b_im.astype(F32)
    bbar_re = fac_re[..., None] * br - fac_im[..., None] * bi
    bbar_im = fac_re[..., None] * bi + fac_im[..., None] * br

    groups = a_re.shape[0]
    tiles = groups // SSM_G_TILE
    gp = SCAN_COLS // SSM_STATE

    bb = jnp.stack([bbar_re, bbar_im], axis=1).astype(BF16)
    bb = bb.reshape(tiles, SSM_G_TILE, 2, SSM_STATE, SSM_GROUP).transpose(0, 1, 4, 2, 3)
    bm = jnp.broadcast_to(bb[:, :, :, :, None, :],
                          (tiles, SSM_G_TILE, SSM_GROUP, 2, gp, SSM_STATE))
    bm = bm.reshape(tiles, SSM_CH_TILE, 2 * SCAN_COLS)
    cc = jnp.stack([c_re.astype(F32), -c_im.astype(F32)], axis=1).astype(BF16)
    cc = cc.reshape(tiles, SSM_G_TILE, 2, SSM_GROUP, SSM_STATE).transpose(0, 2, 4, 1, 3)
    cm = jnp.broadcast_to(cc[:, :, None, :, :, :],
                          (tiles, 2, gp, SSM_STATE, SSM_G_TILE, SSM_GROUP))
    cm = cm.reshape(tiles, 2 * SCAN_COLS, SSM_CH_TILE)
    lbr = lb_re.reshape(tiles, 1, SSM_S_TILE)
    lbi = lb_im.reshape(tiles, 1, SSM_S_TILE)
    return bm, cm, lbr, lbi


def _ssm_layer(x, h0r, h0i, g_pre, g_post, w_in, a_re, a_im, log_step, b_re, b_im,
               c_re, c_im, d_skip, w_glu, w_out, tiles):
    n, t, _ = x.shape
    w = w_glu.shape[0]
    wu = w_in[:, :w].astype(BF16)
    wg = w_in[:, w:].astype(BF16)
    bm, cm, lbr, lbi = _ssm_weights(a_re, a_im, log_step, b_re, b_im, c_re, c_im)
    u_tm, gate = _ssm_in_proj(x, g_pre[None, :], wu, wg, tiles.rows)
    y_tm, htr, hti = _ssm_mixer(u_tm.reshape(t, n, w), h0r.reshape(n, -1), h0i.reshape(n, -1),
                                bm, cm, lbr, lbi, d_skip[None, :], tiles.scan)
    x = _ssm_out_proj(y_tm.reshape(t, n * w), gate, x, w_glu.astype(BF16), w_out.astype(BF16),
                      g_post[None, :], tiles.rows)
    return x, htr.reshape(h0r.shape), hti.reshape(h0i.shape)


def _sb_in_kernel(x_ref, g_ref, wq_ref, wk_ref, wv_ref, wg_ref, *rest):
    q_ref, k_ref, v_ref, gate_ref = rest[-4:]
    xn = _rms(x_ref[...], g_ref[...]).astype(BF16)
    for w_ref, o_ref, act in ((wq_ref, q_ref, None), (wk_ref, k_ref, None), (wv_ref, v_ref, None),
                              (wg_ref, gate_ref, _silu)):
        p = jnp.dot(xn, w_ref[...], preferred_element_type=F32)
        if act is not None:
            p = act(p)
        for h in range(o_ref.shape[0]):
            o_ref[h] = p[:, h * HEAD_DIM:(h + 1) * HEAD_DIM]


def _sb_in_proj(x, g, wq, wk, wv, wg, tc, layer, n_layers, kv_stacks):
    n, t, d = x.shape
    w = wq.shape[1]
    heads = w // HEAD_DIM
    q_spec = pl.BlockSpec((None, heads, tc, HEAD_DIM), lambda i, c: (i, 0, c, 0))
    kv_spec = pl.BlockSpec((None, None, heads, tc, HEAD_DIM), lambda i, c: (layer, i, 0, c, 0))
    kv_shape = jax.ShapeDtypeStruct((n_layers, n, heads, t, HEAD_DIM), F32)
    in_specs = [pl.BlockSpec((None, tc, d), lambda i, c: (i, c, 0)),
                _resident((1, d)), _resident((d, w)), _resident((d, w)),
                _resident((d, w)), _resident((d, w))]
    args = [x, g, wq, wk, wv, wg]
    aliases = {}
    if kv_stacks is not None:
        aliases = {len(args): 1, len(args) + 1: 2}
        in_specs += [pl.BlockSpec(memory_space=pl.ANY)] * 2
        args += list(kv_stacks)
    return pl.pallas_call(
        _sb_in_kernel,
        grid=(n, t // tc),
        in_specs=in_specs,
        out_specs=[q_spec, kv_spec, kv_spec, q_spec],
        out_shape=[jax.ShapeDtypeStruct((n, heads, t, HEAD_DIM), F32), kv_shape, kv_shape,
                   jax.ShapeDtypeStruct((n, heads, t, HEAD_DIM), F32)],
        input_output_aliases=aliases,
        compiler_params=_compiler_params(("parallel", "parallel"), 56),
        name="sb_in_proj",
    )(*args)


def _strict_lower_ones(size):
    j = lax.broadcasted_iota(jnp.int32, (2 * size, size), 0)
    s = lax.broadcasted_iota(jnp.int32, (2 * size, size), 1)
    return jnp.where(jnp.where(j >= size, j - size, j) > s, 1.0, 0.0).astype(BF16)


def _sb_tile(q, kb, vb, tri, carry, mask):
    z = lax.dot_general(q, kb, (((1,), (1,)), ((), ())), preferred_element_type=F32)
    neg_abs = lax.bitcast_convert_type(
        lax.bitcast_convert_type(z, jnp.uint32) | jnp.uint32(0x80000000), F32)
    sp = jnp.maximum(z, 0.0) + jnp.log(1.0 + jnp.exp2(neg_abs)) * LOG2_E
    if mask is not None:
        sp = jnp.where(mask, sp, 0.0)
    hi = sp.astype(BF16)
    lo = (sp - hi.astype(F32)).astype(BF16)
    rest = jnp.dot(jnp.concatenate([hi, lo], axis=1), tri, preferred_element_type=F32)
    arg = z - sp - rest
    total = jnp.sum(sp, axis=-1, keepdims=True)
    if carry is not None:
        arg = arg - carry
        total = total + carry
    w = jnp.exp2(arg)
    if mask is not None:
        w = jnp.where(mask, w, 0.0)
    return total, jnp.dot(w.astype(BF16), vb, preferred_element_type=F32)


def _sb_tail(q, carry, rows, k_ref, v_ref, n_blocks, blk, tri, acc_ref):
    def body(j, carry):
        off = pl.multiple_of((n_blocks - 1 - j) * blk, blk)
        kb = k_ref[pl.ds(off, blk), :].astype(BF16)
        vb = v_ref[pl.ds(off, blk), :].astype(BF16)
        carry, part = _sb_tile(q, kb, vb, tri, carry, None)
        acc_ref[rows, :] += part
        return carry

    return lax.fori_loop(0, n_blocks, body, carry)


def _sb_attn_kernel(*refs, bq, bk, past, emit_state):
    if emit_state:
        *refs, acc_out_ref, carry_out_ref, acc_ref = refs
        refs.append(acc_ref)
    if past:
        q_ref, kn_ref, vn_ref, kp_ref, vp_ref, gate_ref, o_ref, acc_ref = refs
    else:
        q_ref, kn_ref, vn_ref, gate_ref, o_ref, acc_ref = refs
    t = q_ref.shape[0]
    nq = t // bq
    n_past = past // bk
    tri_new = _strict_lower_ones(bq)
    tri_past = tri_new if bk == bq else _strict_lower_ones(bk)
    row = lax.broadcasted_iota(jnp.int32, (bq, bq), 0)
    col = lax.broadcasted_iota(jnp.int32, (bq, bq), 1)
    earlier = col < row

    casts = {}

    def kv_block(k_ref, v_ref, b, blk):
        key = (id(k_ref), b)
        if key not in casts:
            rows = slice(b * blk, (b + 1) * blk)
            casts[key] = (k_ref[rows, :].astype(BF16), v_ref[rows, :].astype(BF16))
        return casts[key]

    tails = []
    for i in range(nq):
        rows = slice(i * bq, (i + 1) * bq)
        q = (q_ref[rows, :] * (HEAD_DIM ** -0.5 * LOG2_E)).astype(BF16)
        new_left, past_left = i + 1, n_past
        carry = acc = None
        for _ in range(SB_STATIC_BLOCKS):
            if new_left:
                new_left -= 1
                kb, vb = kv_block(kn_ref, vn_ref, new_left, bq)
                carry, part = _sb_tile(q, kb, vb, tri_new, carry, earlier if new_left == i else None)
            elif past_left:
                past_left -= 1
                kb, vb = kv_block(kp_ref, vp_ref, past_left, bk)
                carry, part = _sb_tile(q, kb, vb, tri_past, carry, None)
            else:
                break
            acc = part if acc is None else acc + part
        acc_ref[rows, :] = acc
        if new_left or past_left:
            tails.append((rows, q, carry, new_left, past_left))
        if emit_state:
            acc_out_ref[rows, :] = acc
            carry_out_ref[rows, :] = carry

    assert not (emit_state and tails)
    if tails:
        lowest = functools.reduce(jnp.minimum, [c for _, _, c, _, _ in tails])

        @pl.when(jnp.min(lowest) < SB_SKIP_CARRY)
        def _():
            for rows, q, carry, new_left, past_left in tails:
                @pl.when(jnp.min(carry) < SB_SKIP_CARRY)
                def _(rows=rows, q=q, carry=carry, new_left=new_left, past_left=past_left):
                    if new_left:
                        carry = _sb_tail(q, carry, rows, kn_ref, vn_ref, new_left, bq, tri_new, acc_ref)
                    if past_left:
                        _sb_tail(q, carry, rows, kp_ref, vp_ref, past_left, bk, tri_past, acc_ref)

    o_ref[...] = (acc_ref[...] * gate_ref[...]).astype(BF16)


def _sb_rest_kernel(q_ref, acc_in_ref, carry_ref, kp_ref, vp_ref, gate_ref, o_ref, acc_ref, *,
                    bk, n_blocks):
    acc_ref[...] = acc_in_ref[...]
    carry = carry_ref[...]

    @pl.when(jnp.min(carry) < SB_SKIP_CARRY)
    def _():
        q = (q_ref[...] * (HEAD_DIM ** -0.5 * LOG2_E)).astype(BF16)
        _sb_tail(q, carry, slice(None), kp_ref, vp_ref, n_blocks, bk, _strict_lower_ones(bk), acc_ref)

    o_ref[...] = (acc_ref[...] * gate_ref[...]).astype(BF16)


def _sb_attention_rest(q, acc, carry, k_past, v_past, gate_act, layer, bk, n_blocks):
    n, heads, t, dh = q.shape
    past = k_past.shape[3]
    q_spec = pl.BlockSpec((None, None, t, dh), lambda i, h: (i, h, 0, 0))
    carry_spec = pl.BlockSpec((None, None, t, 1), lambda i, h: (i, h, 0, 0))
    past_spec = pl.BlockSpec((None, None, None, past, dh), lambda i, h: (layer, i, h, 0, 0))
    return pl.pallas_call(
        functools.partial(_sb_rest_kernel, bk=bk, n_blocks=n_blocks),
        grid=(n, heads),
        in_specs=[q_spec, q_spec, carry_spec, past_spec, past_spec, q_spec],
        out_specs=pl.BlockSpec((None, t, dh), lambda i, h: (i, 0, h)),
        out_shape=jax.ShapeDtypeStruct((n, t, heads * dh), BF16),
        scratch_shapes=[pltpu.VMEM((t, dh), F32)],
        compiler_params=_compiler_params(("parallel", "parallel"), 48),
        name="sb_attention_rest",
    )(q, acc, carry, k_past, v_past, gate_act)


def _sb_attention(q, k, v, k_past, v_past, gate_act, layer, bq, bk):
    n, heads, t, dh = q.shape
    past = 0 if k_past is None else k_past.shape[3]
    assert t % bq == 0 and past % bk == 0 and (not past or t == bq)
    n_past = past // bk
    q_spec = pl.BlockSpec((None, None, t, dh), lambda i, h: (i, h, 0, 0))
    kv_spec = pl.BlockSpec((None, None, None, t, dh), lambda i, h: (layer, i, h, 0, 0))
    o_spec = pl.BlockSpec((None, t, dh), lambda i, h: (i, 0, h))
    o_shape = jax.ShapeDtypeStruct((n, t, heads * dh), BF16)
    in_specs = [q_spec, kv_spec, kv_spec]
    args = [q, k, v]
    if past:
        last_spec = pl.BlockSpec((None, None, None, bk, dh), lambda i, h: (layer, i, h, n_past - 1, 0))
        in_specs += [last_spec, last_spec]
        args += [k_past, v_past]
    in_specs.append(q_spec)
    args.append(gate_act)
    out_specs, out_shape = [o_spec], [o_shape]
    if past:
        out_specs += [q_spec, pl.BlockSpec((None, None, t, 1), lambda i, h: (i, h, 0, 0))]
        out_shape += [jax.ShapeDtypeStruct((n, heads, t, dh), F32),
                      jax.ShapeDtypeStruct((n, heads, t, 1), F32)]
    kernel = functools.partial(_sb_attn_kernel, bq=bq, bk=bk, past=min(past, bk),
                               emit_state=bool(past))
    out = pl.pallas_call(
        kernel,
        grid=(n, heads),
        in_specs=in_specs,
        out_specs=out_specs,
        out_shape=out_shape,
        scratch_shapes=[pltpu.VMEM((t, dh), F32)],
        compiler_params=_compiler_params(("parallel", "parallel"), 48),
        name="sb_attention",
    )(*args)
    if n_past <= 1:
        return out[0]
    o, acc, carry = out
    return lax.cond(jnp.min(carry) < SB_SKIP_CARRY,
                    lambda: _sb_attention_rest(q, acc, carry, k_past, v_past, gate_act, layer, bk,
                                               n_past - 1),
                    lambda: o)


def _sb_out_kernel(a_ref, x_ref, wout_ref, g_ref, o_ref):
    o = jnp.dot(a_ref[...], wout_ref[...], preferred_element_type=F32)
    o_ref[...] = x_ref[...] + _rms(o, g_ref[...])


def _sb_out_proj(a, x, wout, g, tc):
    n, t, d = x.shape
    w = a.shape[2]
    return pl.pallas_call(
        _sb_out_kernel,
        grid=(n, t // tc),
        in_specs=[pl.BlockSpec((None, tc, w), lambda i, c: (i, c, 0)),
                  pl.BlockSpec((None, tc, d), lambda i, c: (i, c, 0)),
                  _resident((w, d)), _resident((1, d))],
        out_specs=pl.BlockSpec((None, tc, d), lambda i, c: (i, c, 0)),
        out_shape=jax.ShapeDtypeStruct((n, t, d), F32),
        compiler_params=_compiler_params(("parallel", "parallel"), 48),
        name="sb_out_proj",
    )(a, x, wout, g)


def _sb_layer(x, k_past, v_past, kv_stacks, layer, n_layers, g_pre, g_post, w_in, w_out,
              tiles):
    w = w_out.shape[0]
    wq, wk, wv, wg = (w_in[:, i * w:(i + 1) * w].astype(BF16) for i in range(4))
    q, k, v, gate_act = _sb_in_proj(x, g_pre[None, :], wq, wk, wv, wg, tiles.sb_in_rows, layer,
                                    n_layers, kv_stacks)
    a = _sb_attention(q, k, v, k_past, v_past, gate_act, layer, tiles.queries, tiles.keys)
    x = _sb_out_proj(a, x, w_out.astype(BF16), g_post[None, :], tiles.rows)
    return x, (k, v)


class _TrunkTiles(typing.NamedTuple):
    rows: int
    sb_in_rows: int
    scan: int
    queries: int
    keys: int


def _trunk_tiles(t):
    return _TrunkTiles(rows=min(512, t), sb_in_rows=min(256, t), scan=min(256, t),
                       queries=min(256, t), keys=256)


def _run_trunk(x, h0_re, h0_im, k_past, v_past, p):
    depth = p["norm_pre"].shape[0]
    n_sb = p["w_in_sb"].shape[0]
    tiles = _trunk_tiles(x.shape[1])
    kv_stacks = None
    new_re, new_im = [], []
    for i in range(depth):
        j = i // 2
        if i % 2 == 0:
            x, hr, hi = _ssm_layer(
                x, h0_re[j], h0_im[j], p["norm_pre"][i], p["norm_post"][i], p["w_in_ssm"][j],
                p["ssm_a_re"][j], p["ssm_a_im"][j], p["ssm_log_step"][j], p["ssm_b_re"][j],
                p["ssm_b_im"][j], p["ssm_c_re"][j], p["ssm_c_im"][j], p["ssm_d"][j],
                p["w_glu"][j], p["w_out_ssm"][j], tiles)
            new_re.append(hr)
            new_im.append(hi)
        else:
            x, kv_stacks = _sb_layer(x, k_past, v_past, kv_stacks, j, n_sb, p["norm_pre"][i],
                                     p["norm_post"][i], p["w_in_sb"][j], p["w_out_sb"][j], tiles)
    return x, kv_stacks[0], kv_stacks[1], jnp.stack(new_re), jnp.stack(new_im)


def kernel(x_prompt, x_sample, cache_sb_k, cache_sb_v, state_ssm_re, state_ssm_im, norm_pre, norm_post, w_in_ssm, ssm_a_re, ssm_a_im, ssm_log_step, ssm_b_re, ssm_b_im, ssm_c_re, ssm_c_im, ssm_d, w_glu, w_out_ssm, w_in_sb, w_out_sb):
    p = dict(norm_pre=norm_pre, norm_post=norm_post, w_in_ssm=w_in_ssm, ssm_a_re=ssm_a_re,
             ssm_a_im=ssm_a_im, ssm_log_step=ssm_log_step, ssm_b_re=ssm_b_re, ssm_b_im=ssm_b_im,
             ssm_c_re=ssm_c_re, ssm_c_im=ssm_c_im, ssm_d=ssm_d, w_glu=w_glu,
             w_out_ssm=w_out_ssm, w_in_sb=w_in_sb, w_out_sb=w_out_sb)
    nb = x_prompt.shape[0]
    n_ssm = state_ssm_re.shape[0]
    zeros = jnp.zeros((n_ssm, nb) + state_ssm_re.shape[2:], F32)
    y_p, k_p, v_p, re_p, im_p = _run_trunk(x_prompt, zeros, zeros, None, None, p)
    y_s, k_s, v_s, re_s, im_s = _run_trunk(x_sample, state_ssm_re, state_ssm_im,
                                           cache_sb_k, cache_sb_v, p)
    return (y_p, y_s, k_p, v_p, re_p, im_p, k_s, v_s, re_s, im_s)
```
